```python
import math
import jax, jax.numpy as jnp
from jax import lax
import numpy as np

D_MODEL = 1024
BATCH = 4
SEQ = 4096
DEPTH = 2
DEC_BATCH = 128
DEC_SEQ = 4
PAST_LEN = 8192
PAGE_SIZE = 128

N_MIXERS = 2
N_ATTN_LAYERS = (DEPTH + 1) // 2
N_SSM_LAYERS = DEPTH // 2
RMS_EPS = 1e-6
N_HEADS = 16
N_KV_HEADS = 4
HEAD_DIM = 64
KV_REP = N_HEADS // N_KV_HEADS
WINDOW = 128
ATTN_BLOCK = WINDOW
NUM_BUCKETS = 32
MAX_DISTANCE = 128
SSM_EXPAND = 2
D_INNER = SSM_EXPAND * D_MODEL
SSM_HEAD_DIM = 64
SSM_HEADS = D_INNER // SSM_HEAD_DIM
SSM_GROUPS = 8
SSM_REP = SSM_HEADS // SSM_GROUPS
D_STATE = 128
SSM_CONV = 4
CONV_DIM = D_INNER + 2 * SSM_GROUPS * D_STATE
SSM_IN_DIM = D_INNER + CONV_DIM + SSM_HEADS
SSD_CHUNK = 128
D_FF = 2816
FFN_CONV = 3

kernel_name = 'hybrid_swa_sink_ssd_convffn_step'


def rmsnorm(x, g):
    xf = x.astype(jnp.float32)
    y = xf * lax.rsqrt(jnp.mean(xf * xf, axis=-1, keepdims=True) + RMS_EPS)
    return (y * g.astype(jnp.float32)).astype(x.dtype)


def causal_dwconv(x, buf, w, b):
    K = w.shape[0]
    L = x.shape[1]
    xp = jnp.concatenate([buf.astype(x.dtype), x], axis=1)
    y = b.astype(x.dtype)
    for k in range(K):
        y = y + xp[:, k:k + L] * w[k]
    return y, xp[:, L:]


def t5_bucket(dist):
    n = jnp.maximum(dist, 0)
    max_exact = NUM_BUCKETS // 2
    nf = jnp.maximum(n, 1).astype(jnp.float32)
    large = max_exact + (jnp.log(nf / max_exact) / math.log(MAX_DISTANCE / max_exact)
                         * (NUM_BUCKETS - max_exact)).astype(jnp.int32)
    large = jnp.minimum(large, NUM_BUCKETS - 1)
    return jnp.where(n < max_exact, n, large)


def rel_bias(dist, table):
    return jnp.transpose(table[t5_bucket(dist)], (2, 0, 1)).astype(jnp.float32)


def sink_attention(q, k, v, bias, mask, sinks):
    s = jnp.einsum('bnqgrd,bnkgd->bngrqk', q, k).astype(jnp.float32) * (HEAD_DIM ** -0.5)
    s = s + bias.reshape(N_KV_HEADS, KV_REP, *bias.shape[1:])
    s = jnp.where(mask[None, :, None, None], s, -jnp.inf)
    sink = sinks.astype(jnp.float32).reshape(N_KV_HEADS, KV_REP)[:, :, None, None]
    m = jnp.maximum(jnp.max(s, axis=-1, keepdims=True), sink)
    p = jnp.exp(s - m)
    p = p / (jnp.sum(p, axis=-1, keepdims=True) + jnp.exp(sink - m))
    return jnp.einsum('bngrqk,bnkgd->bnqgrd', p.astype(v.dtype), v)


def qkv_proj(h, wqkv):
    b, L, _ = h.shape
    qkv = h @ wqkv
    nq = N_HEADS * HEAD_DIM
    nk = N_KV_HEADS * HEAD_DIM
    q = qkv[..., :nq].reshape(b, L, N_KV_HEADS, KV_REP, HEAD_DIM)
    k = qkv[..., nq:nq + nk].reshape(b, L, N_KV_HEADS, HEAD_DIM)
    v = qkv[..., nq + nk:].reshape(b, L, N_KV_HEADS, HEAD_DIM)
    return q, k, v


def attn_prompt(h, wqkv, wo, sinks, table):
    b, L, _ = h.shape
    q, k, v = qkv_proj(h, wqkv)
    nb = L // ATTN_BLOCK
    pad = jnp.zeros((b, ATTN_BLOCK, N_KV_HEADS, HEAD_DIM), k.dtype)
    kp = jnp.concatenate([pad, k], axis=1)
    vp = jnp.concatenate([pad, v], axis=1)

    def band(t):
        tb = t.reshape(b, nb + 1, ATTN_BLOCK, N_KV_HEADS, HEAD_DIM)
        return jnp.concatenate([tb[:, :-1], tb[:, 1:]], axis=2)

    kj = jnp.arange(2 * ATTN_BLOCK)[None, :]
    dist = (jnp.arange(ATTN_BLOCK)[:, None] + ATTN_BLOCK) - kj
    kpos = jnp.arange(nb)[:, None, None] * ATTN_BLOCK - ATTN_BLOCK + kj[None]
    valid = (dist >= 0) & (dist < WINDOW) & (kpos >= 0)
    qb = q.reshape(b, nb, ATTN_BLOCK, N_KV_HEADS, KV_REP, HEAD_DIM)
    o = sink_attention(qb, band(kp), band(vp), rel_bias(dist, table), valid, sinks)
    y = o.reshape(b, L, N_HEADS * HEAD_DIM) @ wo
    return y, kp[:, -WINDOW:], vp[:, -WINDOW:]


def attn_sample(h, k_buf, v_buf, wqkv, wo, sinks, table):
    b, S, _ = h.shape
    wb = k_buf.shape[1]
    q, k, v = qkv_proj(h, wqkv)
    k_all = jnp.concatenate([k_buf.astype(k.dtype), k], axis=1)
    v_all = jnp.concatenate([v_buf.astype(v.dtype), v], axis=1)
    dist = (wb + jnp.arange(S))[:, None] - jnp.arange(wb + S)[None, :]
    valid = (dist >= 0) & (dist < WINDOW)
    o = sink_attention(q[:, None], k_all[:, None], v_all[:, None], rel_bias(dist, table), valid[None], sinks)
    y = o.reshape(b, S, N_HEADS * HEAD_DIM) @ wo
    return y, k_all[:, -wb:], v_all[:, -wb:]


def ssd_scan(x, dt, A, Bm, Cm, h0):
    b, L = x.shape[:2]
    T = SSD_CHUNK if L % SSD_CHUNK == 0 else L
    nc = L // T
    f32 = jnp.float32
    xc = x.astype(f32).reshape(b, nc, T, SSM_GROUPS, SSM_REP, SSM_HEAD_DIM)
    dtc = dt.reshape(b, nc, T, SSM_GROUPS, SSM_REP)
    Bc = Bm.astype(f32).reshape(b, nc, T, SSM_GROUPS, D_STATE)
    Cc = Cm.astype(f32).reshape(b, nc, T, SSM_GROUPS, D_STATE)
    cs = jnp.cumsum(dtc * A, axis=2)
    causal = (jnp.arange(T)[:, None] >= jnp.arange(T)[None, :])[:, :, None, None]
    seg = cs[:, :, :, None] - cs[:, :, None]
    lmat = jnp.exp(jnp.where(causal, seg, -jnp.inf))
    cb = jnp.einsum('bclgn,bcsgn->bclsg', Cc, Bc)
    w = cb[..., None] * lmat * dtc[:, :, None]
    y_diag = jnp.einsum('bclsgr,bcsgrp->bclgrp', w, xc)
    decay = jnp.exp(cs[:, :, -1:] - cs)
    states = jnp.einsum('bcsgn,bcsgr,bcsgrp->bcgrpn', Bc, decay * dtc, xc)
    chunk_decay = jnp.exp(cs[:, :, -1])

    def step(hc, inp):
        st, dec = inp
        return hc * dec[..., None, None] + st, hc

    h_final, h_prev = lax.scan(step, h0.astype(f32),
                               (jnp.moveaxis(states, 1, 0), jnp.moveaxis(chunk_decay, 1, 0)))
    h_prev = jnp.moveaxis(h_prev, 0, 1)
    y_off = jnp.einsum('bclgn,bcgrpn,bclgr->bclgrp', Cc, h_prev, jnp.exp(cs))
    y = (y_diag + y_off).reshape(b, L, SSM_GROUPS, SSM_REP, SSM_HEAD_DIM)
    return y, h_final


def ssd_mixer(h, conv_buf, ssm_state, w_in, conv_w, conv_b, dt_bias, A_log, D_skip, norm_w, w_out):
    b, L, _ = h.shape
    proj = h @ w_in
    z = proj[..., :D_INNER]
    xbc = proj[..., D_INNER:D_INNER + CONV_DIM]
    dt_raw = proj[..., D_INNER + CONV_DIM:]
    xbc_c, new_conv = causal_dwconv(xbc, conv_buf, conv_w, conv_b)
    xbc_c = jax.nn.silu(xbc_c)
    gn = SSM_GROUPS * D_STATE
    xs = xbc_c[..., :D_INNER].reshape(b, L, SSM_GROUPS, SSM_REP, SSM_HEAD_DIM)
    Bm = xbc_c[..., D_INNER:D_INNER + gn].reshape(b, L, SSM_GROUPS, D_STATE)
    Cm = xbc_c[..., D_INNER + gn:].reshape(b, L, SSM_GROUPS, D_STATE)
    dt = jax.nn.softplus(dt_raw.astype(jnp.float32) + dt_bias.astype(jnp.float32))
    dt = dt.reshape(b, L, SSM_GROUPS, SSM_REP)
    A = -jnp.exp(A_log.astype(jnp.float32)).reshape(SSM_GROUPS, SSM_REP)
    h0 = ssm_state.reshape(b, SSM_GROUPS, SSM_REP, SSM_HEAD_DIM, D_STATE)
    y, h_new = ssd_scan(xs, dt, A, Bm, Cm, h0)
    y = y + D_skip.astype(jnp.float32).reshape(SSM_GROUPS, SSM_REP)[:, :, None] * xs.astype(jnp.float32)
    y = y.reshape(b, L, D_INNER) * jax.nn.silu(z.astype(jnp.float32))
    yg = y.reshape(b, L, SSM_GROUPS, D_INNER // SSM_GROUPS)
    yg = yg * lax.rsqrt(jnp.mean(yg * yg, axis=-1, keepdims=True) + RMS_EPS)
    y = (yg.reshape(b, L, D_INNER) * norm_w.astype(jnp.float32)).astype(h.dtype)
    out = y @ w_out
    return out, new_conv, h_new.reshape(b, SSM_HEADS, SSM_HEAD_DIM, D_STATE).astype(ssm_state.dtype)


def conv_ffn(h, buf, w_up, conv_w, conv_b, w_down):
    u = h @ w_up
    u, new_buf = causal_dwconv(u, buf, conv_w, conv_b)
    return (jax.nn.silu(u[..., :D_FF]) * u[..., D_FF:]) @ w_down, new_buf


def run_trunk(x, cache_k, cache_v, st_conv, st_ssm, st_ffn,
              rel_bias_table, norm_mix, norm_ffn, norm_final,
              attn_wqkv, attn_wo, attn_sinks,
              ssm_w_in, ssm_conv_w, ssm_conv_b, ssm_dt_bias, ssm_A_log, ssm_D, ssm_norm, ssm_w_out,
              ffn_w_up, ffn_conv_w, ffn_conv_b, ffn_w_down):
    prompt = cache_k is None
    b = x.shape[0]
    new_k, new_v, new_conv, new_ssm, new_ffn = [], [], [], [], []
    for i in range(DEPTH):
        h = rmsnorm(x, norm_mix[i])
        if i % N_MIXERS == 0:
            a = i // N_MIXERS
            if prompt:
                o, kb, vb = attn_prompt(h, attn_wqkv[a], attn_wo[a], attn_sinks[a], rel_bias_table)
            else:
                o, kb, vb = attn_sample(h, cache_k[a], cache_v[a], attn_wqkv[a], attn_wo[a],
                                        attn_sinks[a], rel_bias_table)
            new_k.append(kb)
            new_v.append(vb)
        else:
            s = i // N_MIXERS
            cbuf = jnp.zeros((b, SSM_CONV - 1, CONV_DIM), x.dtype) if prompt else st_conv[s]
            hst = jnp.zeros((b, SSM_HEADS, SSM_HEAD_DIM, D_STATE), jnp.float32) if prompt else st_ssm[s]
            o, cb, hs = ssd_mixer(h, cbuf, hst, ssm_w_in[s], ssm_conv_w[s], ssm_conv_b[s], ssm_dt_bias[s],
                                  ssm_A_log[s], ssm_D[s], ssm_norm[s], ssm_w_out[s])
            new_conv.append(cb)
            new_ssm.append(hs)
        x = x + o
        h = rmsnorm(x, norm_ffn[i])
        fbuf = jnp.zeros((b, FFN_CONV - 1, 2 * D_FF), x.dtype) if prompt else st_ffn[i]
        o, fb = conv_ffn(h, fbuf, ffn_w_up[i], ffn_conv_w[i], ffn_conv_b[i], ffn_w_down[i])
        new_ffn.append(fb)
        x = x + o
    y = rmsnorm(x, norm_final)
    return (y, jnp.stack(new_k), jnp.stack(new_v), jnp.stack(new_conv), jnp.stack(new_ssm), jnp.stack(new_ffn))


def setup_inputs(seed: int = 0) -> dict:
    key = jax.random.key(seed)
    ks = jax.random.split(key, 32)
    f32 = jnp.float32

    def nrm(k, shape, scale):
        return scale * jax.random.normal(k, shape, f32)

    win_buf = min(WINDOW, PAST_LEN)
    qkv_dim = (N_HEADS + 2 * N_KV_HEADS) * HEAD_DIM
    dt0 = jnp.exp(jax.random.uniform(ks[17], (N_SSM_LAYERS, SSM_HEADS), f32, math.log(1e-3), math.log(1e-1)))
    return {
        'x_prompt': nrm(ks[0], (BATCH, SEQ, D_MODEL), 1.0),
        'x_sample': nrm(ks[1], (DEC_BATCH, DEC_SEQ, D_MODEL), 1.0),
        'cache_k_win': nrm(ks[2], (N_ATTN_LAYERS, DEC_BATCH, win_buf, N_KV_HEADS, HEAD_DIM), 1.0),
        'cache_v_win': nrm(ks[3], (N_ATTN_LAYERS, DEC_BATCH, win_buf, N_KV_HEADS, HEAD_DIM), 1.0),
        'state_ssm_conv': nrm(ks[4], (N_SSM_LAYERS, DEC_BATCH, SSM_CONV - 1, CONV_DIM), 1.0),
        'state_ssm': nrm(ks[5], (N_SSM_LAYERS, DEC_BATCH, SSM_HEADS, SSM_HEAD_DIM, D_STATE), 0.5),
        'state_ffn_conv': nrm(ks[6], (DEPTH, DEC_BATCH, FFN_CONV - 1, 2 * D_FF), 1.0),
        'rel_bias_table': nrm(ks[7], (NUM_BUCKETS, N_HEADS), 0.5),
        'norm_mix': 1.0 + nrm(ks[8], (DEPTH, D_MODEL), 0.02),
        'norm_ffn': 1.0 + nrm(ks[9], (DEPTH, D_MODEL), 0.02),
        'norm_final': 1.0 + nrm(ks[10], (D_MODEL,), 0.02),
        'attn_wqkv': nrm(ks[11], (N_ATTN_LAYERS, D_MODEL, qkv_dim), D_MODEL ** -0.5),
        'attn_wo': nrm(ks[12], (N_ATTN_LAYERS, N_HEADS * HEAD_DIM, D_MODEL), (N_HEADS * HEAD_DIM) ** -0.5),
        'attn_sinks': nrm(ks[13], (N_ATTN_LAYERS, N_HEADS), 0.5),
        'ssm_w_in': nrm(ks[14], (N_SSM_LAYERS, D_MODEL, SSM_IN_DIM), D_MODEL ** -0.5),
        'ssm_conv_w': nrm(ks[15], (N_SSM_LAYERS, SSM_CONV, CONV_DIM), SSM_CONV ** -0.5),
        'ssm_conv_b': nrm(ks[16], (N_SSM_LAYERS, CONV_DIM), 0.01),
        'ssm_dt_bias': dt0 + jnp.log(-jnp.expm1(-dt0)),
        'ssm_A_log': jnp.log(jax.random.uniform(ks[18], (N_SSM_LAYERS, SSM_HEADS), f32, 1.0, 16.0)),
        'ssm_D': 1.0 + nrm(ks[19], (N_SSM_LAYERS, SSM_HEADS), 0.1),
        'ssm_norm': 1.0 + nrm(ks[20], (N_SSM_LAYERS, D_INNER), 0.02),
        'ssm_w_out': nrm(ks[21], (N_SSM_LAYERS, D_INNER, D_MODEL), D_INNER ** -0.5),
        'ffn_w_up': nrm(ks[22], (DEPTH, D_MODEL, 2 * D_FF), D_MODEL ** -0.5),
        'ffn_conv_w': nrm(ks[23], (DEPTH, FFN_CONV, 2 * D_FF), FFN_CONV ** -0.5),
        'ffn_conv_b': nrm(ks[24], (DEPTH, 2 * D_FF), 0.01),
        'ffn_w_down': nrm(ks[25], (DEPTH, D_FF, D_MODEL), D_FF ** -0.5),
    }


def reference(x_prompt, x_sample, cache_k_win, cache_v_win, state_ssm_conv, state_ssm, state_ffn_conv,
              rel_bias_table, norm_mix, norm_ffn, norm_final,
              attn_wqkv, attn_wo, attn_sinks,
              ssm_w_in, ssm_conv_w, ssm_conv_b, ssm_dt_bias, ssm_A_log, ssm_D, ssm_norm, ssm_w_out,
              ffn_w_up, ffn_conv_w, ffn_conv_b, ffn_w_down):
    y_prompt, p_k, p_v, p_conv, p_ssm, p_ffn = run_trunk(
        x_prompt, None, None, None, None, None,
        rel_bias_table, norm_mix, norm_ffn, norm_final, attn_wqkv, attn_wo, attn_sinks,
        ssm_w_in, ssm_conv_w, ssm_conv_b, ssm_dt_bias, ssm_A_log, ssm_D, ssm_norm, ssm_w_out,
        ffn_w_up, ffn_conv_w, ffn_conv_b, ffn_w_down)
    y_sample, s_k, s_v, s_conv, s_ssm, s_ffn = run_trunk(
        x_sample, cache_k_win, cache_v_win, state_ssm_conv, state_ssm, state_ffn_conv,
        rel_bias_table, norm_mix, norm_ffn, norm_final, attn_wqkv, attn_wo, attn_sinks,
        ssm_w_in, ssm_conv_w, ssm_conv_b, ssm_dt_bias, ssm_A_log, ssm_D, ssm_norm, ssm_w_out,
        ffn_w_up, ffn_conv_w, ffn_conv_b, ffn_w_down)
    return (y_prompt, y_sample, p_k, p_v, p_conv, p_ssm, p_ffn, s_k, s_v, s_conv, s_ssm, s_ffn)
```

```python
import functools
import math

import jax
import jax.numpy as jnp
from jax import lax
from jax.experimental import pallas as pl
from jax.experimental.pallas import tpu as pltpu

F32 = jnp.float32
BF16 = jnp.bfloat16

RMS_EPS = 1e-6
N_HEADS = 16
N_KV_HEADS = 4
HEAD_DIM = 64
KV_REP = N_HEADS // N_KV_HEADS
WINDOW = 128
NUM_BUCKETS = 32
MAX_DISTANCE = 128
SSM_HEAD_DIM = 64
SSM_GROUPS = 8
D_STATE = 128
SSD_CHUNK = 128

VMEM_LIMIT_BYTES = 56 * 1024 * 1024
SUBLANES = 8
LANES = 128
ROW_TILE = 512


def _params(*semantics):
    return pltpu.CompilerParams(dimension_semantics=semantics, vmem_limit_bytes=VMEM_LIMIT_BYTES)


def _resident(shape):
    return pl.BlockSpec(shape, lambda *_: (0,) * len(shape), pipeline_mode=pl.Buffered(1))


def _rmsnorm(x, g):
    return x * lax.rsqrt(jnp.mean(x * x, axis=-1, keepdims=True) + RMS_EPS) * g


def _silu(x):
    return x * jax.nn.sigmoid(x)


def _dot(a, b):
    return jnp.dot(a, b, preferred_element_type=F32)


def _dot_nt(a, b):
    return lax.dot_general(a, b, (((1,), (1,)), ((), ())), preferred_element_type=F32)


def _dot_tn(a, b):
    return lax.dot_general(a, b, (((0,), (0,)), ((), ())), preferred_element_type=F32)


def _split3(x):
    hi = x.astype(BF16)
    r1 = x - hi.astype(F32)
    mid = r1.astype(BF16)
    lo = (r1 - mid.astype(F32)).astype(BF16)
    return hi, mid, lo


def _dot_exact_rhs(x, sel):
    hi, mid, lo = _split3(x)
    return _dot(hi, sel) + _dot(mid, sel) + _dot(lo, sel)


def _dot_exact_lhs(sel, x):
    hi, mid, lo = _split3(x)
    return _dot(sel, hi) + _dot(sel, mid) + _dot(sel, lo)


def _norm_proj_kernel(x_ref, g_ref, w_ref, *out_refs):
    h = _rmsnorm(x_ref[...], g_ref[...]).astype(BF16)
    c0 = 0
    for o_ref in out_refs:
        c1 = c0 + o_ref.shape[1]
        o_ref[...] = _dot(h, w_ref[:, c0:c1]).astype(o_ref.dtype)
        c0 = c1


def norm_proj(x, g, w, out_cols_dtypes):
    m, d = x.shape
    tm = min(ROW_TILE, m)
    return pl.pallas_call(
        _norm_proj_kernel,
        grid=(m // tm,),
        in_specs=[pl.BlockSpec((tm, d), lambda i: (i, 0)), _resident((1, d)), _resident(w.shape)],
        out_specs=[pl.BlockSpec((tm, n), lambda i: (i, 0)) for n, _ in out_cols_dtypes],
        out_shape=[jax.ShapeDtypeStruct((m, n), dt) for n, dt in out_cols_dtypes],
        compiler_params=_params("arbitrary"),
        name="norm_proj",
    )(x, g.reshape(1, d), w)


def _proj_residual_kernel(a_ref, w_ref, x_ref, o_ref):
    o_ref[...] = x_ref[...] + _dot(a_ref[...].astype(BF16), w_ref[...])


def proj_residual(a, w, x):
    m, k = a.shape
    d = w.shape[1]
    tm = min(ROW_TILE, m)
    return pl.pallas_call(
        _proj_residual_kernel,
        grid=(m // tm,),
        in_specs=[pl.BlockSpec((tm, k), lambda i: (i, 0)), _resident(w.shape),
                  pl.BlockSpec((tm, d), lambda i: (i, 0))],
        out_specs=pl.BlockSpec((tm, d), lambda i: (i, 0)),
        out_shape=jax.ShapeDtypeStruct((m, d), F32),
        compiler_params=_params("arbitrary"),
        name="proj_residual",
    )(a, w, x)


def _conv_taps(ubuf, cols, cw_ref, cb_ref, c0, width, *, tm, halo, stride, taps):
    y = cb_ref[:, c0:c0 + width]
    for k in range(taps):
        off = halo - (taps - 1 - k) * stride
        y = y + ubuf[off:off + tm, cols] * cw_ref[k:k + 1, c0:c0 + width]
    return y


def _conv_ffn_kernel(*refs, tm, halo, stride, taps, tf, d_ff, has_state, final_norm):
    it = iter(refs)
    x_ref, g_ref, wup_ref, cw_ref, cb_ref, wdn_ref = (next(it) for _ in range(6))
    init_ref = next(it) if has_state else None
    gfin_ref = next(it) if final_norm else None
    o_ref, st_ref, ubuf = next(it), next(it), next(it)

    @pl.when(pl.program_id(1) == 0)
    def _():
        st_ref[0] = init_ref[0] if has_state else jnp.zeros(st_ref.shape[1:], F32)

    x = x_ref[...]
    h = _rmsnorm(x, g_ref[...]).astype(BF16)
    acc = None
    for j in range(d_ff // tf):
        conv = []
        for half in range(2):
            c0 = half * d_ff + j * tf
            cols = slice(half * tf, (half + 1) * tf)
            ubuf[0:halo, cols] = st_ref[0, :, c0:c0 + tf]
            ubuf[halo:halo + tm, cols] = _dot(h, wup_ref[:, c0:c0 + tf])
            conv.append(_conv_taps(ubuf, cols, cw_ref, cb_ref, c0, tf,
                                   tm=tm, halo=halo, stride=stride, taps=taps))
            st_ref[0, :, c0:c0 + tf] = ubuf[tm:tm + halo, cols]
        a = (_silu(conv[0]) * conv[1]).astype(BF16)
        part = _dot(a, wdn_ref[j * tf:(j + 1) * tf, :])
        acc = part if acc is None else acc + part
    out = x + acc
    if final_norm:
        out = _rmsnorm(out, gfin_ref[...])
    o_ref[...] = out


def conv_ffn(x, g, w_up, conv_w, conv_b, w_down, *, n_seq, stride, state=None, g_final=None):
    m, d = x.shape
    d_ff = w_down.shape[0]
    taps = conv_w.shape[0]
    rows = m // n_seq
    tm = min(ROW_TILE, rows)
    halo = max(SUBLANES, (taps - 1) * stride)
    tf = 256
    kern = functools.partial(_conv_ffn_kernel, tm=tm, halo=halo, stride=stride, taps=taps, tf=tf,
                             d_ff=d_ff, has_state=state is not None, final_norm=g_final is not None)
    tiles = rows // tm
    row_spec = pl.BlockSpec((tm, d), lambda s, t: (s * tiles + t, 0))
    st_spec = pl.BlockSpec((1, halo, 2 * d_ff), lambda s, t: (s, 0, 0))
    in_specs = [row_spec, _resident((1, d)), _resident(w_up.shape), _resident(conv_w.shape),
                _resident((1, 2 * d_ff)), _resident(w_down.shape)]
    args = [x, g.reshape(1, d), w_up, conv_w, conv_b.reshape(1, 2 * d_ff), w_down]
    if state is not None:
        in_specs.append(st_spec)
        args.append(state)
    if g_final is not None:
        in_specs.append(_resident((1, d)))
        args.append(g_final.reshape(1, d))
    return pl.pallas_call(
        kern,
        grid=(n_seq, tiles),
        in_specs=in_specs,
        out_specs=[row_spec, st_spec],
        out_shape=[jax.ShapeDtypeStruct((m, d), F32),
                   jax.ShapeDtypeStruct((n_seq, halo, 2 * d_ff), F32)],
        scratch_shapes=[pltpu.VMEM((halo + tm, 2 * tf), F32)],
        compiler_params=_params("arbitrary", "arbitrary"),
        name="conv_ffn",
    )(*args)


def _softplus(x):
    return jnp.maximum(x, 0.0) + jnp.log1p(jnp.exp(-jnp.abs(x)))


def _ssd_in_proj_kernel(*refs, tm, halo, stride, taps, tc, has_state):
    it = iter(refs)
    x_ref, g_ref, wz_ref, wx_ref, wdt_ref, cw_ref, cb_ref, dtb_ref = (next(it) for _ in range(8))
    init_ref = next(it) if has_state else None
    z_ref, xs_ref, b_ref, c_ref, dt_ref, st_ref, ubuf = (next(it) for _ in range(7))

    @pl.when(pl.program_id(1) == 0)
    def _():
        st_ref[0] = init_ref[0] if has_state else jnp.zeros(st_ref.shape[1:], F32)

    h = _rmsnorm(x_ref[...], g_ref[...]).astype(BF16)
    z_ref[...] = _dot(h, wz_ref[...])
    dt_ref[...] = _softplus(_dot(h, wdt_ref[...]) + dtb_ref[...])
    outs = ((xs_ref, 0), (b_ref, xs_ref.shape[1]), (c_ref, xs_ref.shape[1] + b_ref.shape[1]))
    for j in range(wx_ref.shape[1] // tc):
        c0 = j * tc
        ubuf[0:halo, :] = st_ref[0, :, c0:c0 + tc]
        ubuf[halo:halo + tm, :] = _dot(h, wx_ref[:, c0:c0 + tc])
        y = _silu(_conv_taps(ubuf, slice(None), cw_ref, cb_ref, c0, tc,
                             tm=tm, halo=halo, stride=stride, taps=taps))
        st_ref[0, :, c0:c0 + tc] = ubuf[tm:tm + halo, :]
        o_ref, base = [(r, b0) for r, b0 in outs if b0 <= c0][-1]
        o_ref[:, c0 - base:c0 - base + tc] = y.astype(o_ref.dtype)


def ssd_in_proj(x, g, w_z, w_xbc, w_dt, conv_w, conv_b, dt_bias, *, d_inner, n_seq, stride, state=None):
    m, d = x.shape
    conv_dim = w_xbc.shape[1]
    gn = (conv_dim - d_inner) // 2
    taps = conv_w.shape[0]
    rows = m // n_seq
    tm = min(ROW_TILE, rows)
    halo = max(SUBLANES, (taps - 1) * stride)
    tc = 512
    kern = functools.partial(_ssd_in_proj_kernel, tm=tm, halo=halo, stride=stride, taps=taps, tc=tc,
                             has_state=state is not None)
    tiles = rows // tm

    def rows_of(n):
        return pl.BlockSpec((tm, n), lambda s, t: (s * tiles + t, 0))

    st_spec = pl.BlockSpec((1, halo, conv_dim), lambda s, t: (s, 0, 0))
    in_specs = [rows_of(d), _resident((1, d)), _resident(w_z.shape), _resident(w_xbc.shape),
                _resident(w_dt.shape), _resident(conv_w.shape), _resident((1, conv_dim)),
                _resident((1, LANES))]
    args = [x, g.reshape(1, d), w_z, w_xbc, w_dt, conv_w, conv_b.reshape(1, conv_dim), dt_bias]
    if state is not None:
        in_specs.append(st_spec)
        args.append(state)
    return pl.pallas_call(
        kern,
        grid=(n_seq, tiles),
        in_specs=in_specs,
        out_specs=[rows_of(d_inner), rows_of(d_inner), rows_of(gn), rows_of(gn), rows_of(LANES), st_spec],
        out_shape=[jax.ShapeDtypeStruct((m, d_inner), F32), jax.ShapeDtypeStruct((m, d_inner), F32),
                   jax.ShapeDtypeStruct((m, gn), BF16), jax.ShapeDtypeStruct((m, gn), BF16),
                   jax.ShapeDtypeStruct((m, LANES), F32),
                   jax.ShapeDtypeStruct((n_seq, halo, conv_dim), F32)],
        scratch_shapes=[pltpu.VMEM((halo + tm, tc), F32)],
        compiler_params=_params("arbitrary", "arbitrary"),
        name="ssd_in_proj",
    )(*args)


def _block_diag_rows(a, rep, width):
    col_block = lax.broadcasted_iota(jnp.int32, a.shape, 1) // width
    zero = jnp.zeros_like(a)
    return jnp.concatenate([jnp.where(col_block == r, a, zero) for r in range(rep)], axis=0)


def _head_expander(n_heads, width):
    row = lax.broadcasted_iota(jnp.int32, (LANES, n_heads * width), 0)
    col = lax.broadcasted_iota(jnp.int32, (LANES, n_heads * width), 1)
    return jnp.where(row == col // width, 1.0, 0.0).astype(BF16)


def _ssd_prompt_kernel(z_ref, xs_ref, b_ref, c_ref, dt_ref, alog_ref, dexp_ref, nw_ref,
                       y_ref, h_ref, *, groups, rep, hd, n_state):
    t = dt_ref.shape[0]
    gw = rep * hd

    @pl.when(pl.program_id(1) == 0)
    def _():
        h_ref[...] = jnp.zeros(h_ref.shape, F32)

    dt = dt_ref[...]
    da = dt * (-jnp.exp(alog_ref[...]))
    row = lax.broadcasted_iota(jnp.int32, (t, t), 0)
    col = lax.broadcasted_iota(jnp.int32, (t, t), 1)
    causal = row >= col
    cs = _dot_exact_lhs(jnp.where(causal, 1.0, 0.0).astype(BF16), da)
    cs_t = cs.T
    dt_t = dt.T
    dd_t = jnp.exp(cs_t[:, t - 1:t] - cs_t) * dt_t
    chunk_decay = jnp.exp(jnp.broadcast_to(cs[t - 1:t, :], (SUBLANES, LANES)))
    cdx = _dot_exact_rhs(chunk_decay, _head_expander(groups * rep, hd))[0:1, :]

    for g in range(groups):
        gs = slice(g * gw, (g + 1) * gw)
        ns = slice(g * n_state, (g + 1) * n_state)
        bg = b_ref[:, ns]
        cg = c_ref[:, ns]
        cb = _dot_nt(cg, bg)
        b_t = bg.astype(F32).T
        cg32 = cg.astype(F32)
        xg = xs_ref[:, gs]
        x_bd = _block_diag_rows(xg.astype(BF16), rep, hd)
        hg = h_ref[0, g]
        h_bd = _block_diag_rows(hg.astype(BF16), rep, hd)
        w_parts, c_parts, s_parts = [], [], []
        for r in range(rep):
            hh = g * rep + r
            cs_col = cs[:, hh:hh + 1]
            seg = cs_col - cs_t[hh:hh + 1, :]
            lmat = jnp.exp(jnp.where(causal, seg, -jnp.inf))
            w_parts.append((cb * lmat * dt_t[hh:hh + 1, :]).astype(BF16))
            c_parts.append((cg32 * jnp.exp(cs_col)).astype(BF16))
            s_parts.append((b_t * dd_t[hh:hh + 1, :]).astype(BF16))
        y = _dot(jnp.concatenate(w_parts + c_parts, axis=1), jnp.concatenate([x_bd, h_bd], axis=0))
        h_ref[0, g] = hg * cdx[:, gs] + _dot(jnp.concatenate(s_parts, axis=1), x_bd)
        y = (y + dexp_ref[:, gs] * xg) * _silu(z_ref[:, gs])
        y = y * lax.rsqrt(jnp.mean(y * y, axis=-1, keepdims=True) + RMS_EPS) * nw_ref[:, gs]
        y_ref[:, gs] = y.astype(y_ref.dtype)


def ssd_prompt(z, xs, bm, cm, dt, a_log, d_exp, norm_w, *, n_seq):
    m, d_inner = xs.shape
    gn = bm.shape[1]
    groups, n_state, hd = SSM_GROUPS, D_STATE, SSM_HEAD_DIM
    rep = d_inner // (groups * hd)
    t = SSD_CHUNK
    chunks = m // n_seq // t
    kern = functools.partial(_ssd_prompt_kernel, groups=groups, rep=rep, hd=hd, n_state=n_state)

    def rows_of(n):
        return pl.BlockSpec((t, n), lambda s, c: (s * chunks + c, 0))

    h_spec = pl.BlockSpec((1, groups, n_state, rep * hd), lambda s, c: (s, 0, 0, 0))
    return pl.pallas_call(
        kern,
        grid=(n_seq, chunks),
        in_specs=[rows_of(d_inner), rows_of(d_inner), rows_of(gn), rows_of(gn), rows_of(LANES),
                  _resident((1, LANES)), _resident((1, d_inner)), _resident((1, d_inner))],
        out_specs=[rows_of(d_inner), h_spec],
        out_shape=[jax.ShapeDtypeStruct((m, d_inner), BF16),
                   jax.ShapeDtypeStruct((n_seq, groups, n_state, rep * hd), F32)],
        compiler_params=_params("arbitrary", "arbitrary"),
        name="ssd_prompt",
    )(z, xs, bm, cm, dt, a_log, d_exp, norm_w)


def _bias_from_buckets(bucket, table_ref, head):
    acc = jnp.zeros(bucket.shape, F32)
    for j in range(NUM_BUCKETS):
        acc = jnp.where(bucket == j, table_ref[j, head], acc)
    return jnp.where(bucket >= 0, acc, -jnp.inf)


def _sink_softmax(s, sink):
    m = jnp.maximum(jnp.max(s, axis=-1, keepdims=True), sink)
    p = jnp.exp(s - m)
    return p / (jnp.sum(p, axis=-1, keepdims=True) + jnp.exp(sink - m))


def _attn_prompt_kernel(table_ref, sinks_ref, bucket_ref, q_ref, kvp_ref, kvo_ref, o_ref, bias_ref):
    blk = q_ref.shape[0]
    first_block = pl.program_id(1) == 0

    @pl.when((pl.program_id(0) == 0) & first_block)
    def _():
        bucket = bucket_ref[...]
        for hh in range(N_HEADS):
            bias_ref[hh] = _bias_from_buckets(bucket, table_ref, hh)

    kv = jnp.concatenate([kvp_ref[...], kvo_ref[...]], axis=0).astype(BF16)
    key = lax.broadcasted_iota(jnp.int32, (1, 2 * blk), 1)
    has_prev = (key >= blk) | jnp.logical_not(first_block)
    scale = HEAD_DIM ** -0.5
    gw = KV_REP * HEAD_DIM
    for g in range(N_KV_HEADS):
        kg = kv[:, g * HEAD_DIM:(g + 1) * HEAD_DIM]
        vg = kv[:, N_KV_HEADS * HEAD_DIM + g * HEAD_DIM:N_KV_HEADS * HEAD_DIM + (g + 1) * HEAD_DIM]
        heads = range(g * KV_REP, (g + 1) * KV_REP)
        qg = jnp.concatenate([q_ref[:, hh * HEAD_DIM:(hh + 1) * HEAD_DIM] for hh in heads], axis=0)
        s = _dot_nt(qg, kg) * scale
        s = s + bias_ref[g * KV_REP:(g + 1) * KV_REP].reshape(KV_REP * blk, 2 * blk)
        s = jnp.where(has_prev, s, -jnp.inf)
        sink = jnp.concatenate([jnp.full((blk, 1), sinks_ref[hh], F32) for hh in heads], axis=0)
        p = _sink_softmax(s, sink).astype(BF16)
        p_wide = jnp.concatenate([p[r * blk:(r + 1) * blk] for r in range(KV_REP)], axis=1)
        v_bd = _block_diag_rows(jnp.concatenate([vg] * KV_REP, axis=1), KV_REP, HEAD_DIM)
        o_ref[:, g * gw:(g + 1) * gw] = _dot(p_wide, v_bd).astype(o_ref.dtype)


def _t5_bucket(dist):
    max_exact = NUM_BUCKETS // 2
    nf = jnp.maximum(dist, 1).astype(F32)
    large = max_exact + (jnp.log(nf / max_exact) / math.log(MAX_DISTANCE / max_exact)
                         * (NUM_BUCKETS - max_exact)).astype(jnp.int32)
    large = jnp.minimum(large, NUM_BUCKETS - 1)
    return jnp.where(dist < max_exact, dist, large)


def attn_prompt(q, kv, table, sinks, *, n_seq):
    m = q.shape[0]
    blk = WINDOW
    nb = m // n_seq // blk
    dist = (jnp.arange(blk)[:, None] + blk) - jnp.arange(2 * blk)[None, :]
    bucket = jnp.where((dist >= 0) & (dist < WINDOW), _t5_bucket(jnp.maximum(dist, 0)), -1).astype(jnp.int32)
    smem = pl.BlockSpec(memory_space=pltpu.SMEM)
    return pl.pallas_call(
        _attn_prompt_kernel,
        grid=(n_seq, nb),
        in_specs=[smem, smem, _resident(bucket.shape),
                  pl.BlockSpec((blk, q.shape[1]), lambda s, c: (s * nb + c, 0)),
                  pl.BlockSpec((blk, kv.shape[1]), lambda s, c: (s * nb + jnp.maximum(c - 1, 0), 0)),
                  pl.BlockSpec((blk, kv.shape[1]), lambda s, c: (s * nb + c, 0))],
        out_specs=pl.BlockSpec((blk, q.shape[1]), lambda s, c: (s * nb + c, 0)),
        out_shape=jax.ShapeDtypeStruct(q.shape, BF16),
        scratch_shapes=[pltpu.VMEM((N_HEADS, blk, 2 * blk), F32)],
        compiler_params=_params("arbitrary", "arbitrary"),
        name="attn_prompt",
    )(table, sinks, bucket, q, kv, kv)


def _attn_sample_kernel(table_ref, bucket_ref, sink_ref, q_ref, kvn_ref, ck_ref, cv_ref,
                        o_ref, nk_ref, nv_ref, bias_ref, kall, vall, *, n_new):
    bb, win, gd = ck_ref.shape
    rows = bias_ref.shape[0]
    per_group = rows // N_KV_HEADS
    pad = kvn_ref.shape[1]

    @pl.when(pl.program_id(0) == 0)
    def _():
        for hh in range(N_HEADS):
            rs = slice(hh * n_new, (hh + 1) * n_new)
            bias_ref[rs, :] = _bias_from_buckets(bucket_ref[rs, :], table_ref, hh)
        kall[win + pad:, :] = jnp.zeros((kall.shape[0] - win - pad, gd), F32)
        vall[win + pad:, :] = jnp.zeros((vall.shape[0] - win - pad, gd), F32)

    row_group = lax.broadcasted_iota(jnp.int32, (rows, gd), 0) // per_group
    col_group = lax.broadcasted_iota(jnp.int32, (rows, gd), 1) // HEAD_DIM
    own = row_group == col_group
    scale = HEAD_DIM ** -0.5
    for i in range(bb):
        kall[0:win, :] = ck_ref[i]
        kall[win:win + pad, :] = kvn_ref[i, :, 0:gd]
        vall[0:win, :] = cv_ref[i]
        vall[win:win + pad, :] = kvn_ref[i, :, gd:2 * gd]
        nk_ref[i] = kall[n_new:n_new + win, :]
        nv_ref[i] = vall[n_new:n_new + win, :]
        qi = q_ref[i]
        q_bd = jnp.where(own, jnp.concatenate([qi] * N_KV_HEADS, axis=0), jnp.zeros((), qi.dtype))
        s = _dot_nt(q_bd, kall[...].astype(BF16)) * scale + bias_ref[...]
        p = _sink_softmax(s, sink_ref[...]).astype(BF16)
        o = jnp.where(own, _dot(p, vall[...].astype(BF16)), 0.0)
        o_ref[i] = sum(o[g * per_group:(g + 1) * per_group] for g in range(N_KV_HEADS)).astype(o_ref.dtype)


def attn_sample(q, kv_new, cache_k, cache_v, table, sinks):
    bsz, win, gd = cache_k.shape
    n_new = kv_new.shape[1]
    rows = N_HEADS * n_new
    keys = 2 * LANES
    pad = SUBLANES
    bb = 8
    kv_new = jnp.pad(kv_new, ((0, 0), (0, pad - n_new), (0, 0)))
    dist = (win + jnp.arange(n_new))[:, None] - jnp.arange(keys)[None, :]
    ok = (dist >= 0) & (dist < WINDOW) & (jnp.arange(keys)[None, :] < win + n_new)
    bucket = jnp.where(ok, _t5_bucket(jnp.maximum(dist, 0)), -1).astype(jnp.int32)
    bucket = jnp.tile(bucket, (N_HEADS, 1))
    sink_rows = jnp.repeat(sinks.astype(F32), n_new).reshape(rows, 1)
    smem = pl.BlockSpec(memory_space=pltpu.SMEM)

    def per_seq(shape):
        return pl.BlockSpec((bb,) + shape, lambda i: (i, 0, 0))

    return pl.pallas_call(
        functools.partial(_attn_sample_kernel, n_new=n_new),
        grid=(bsz // bb,),
        in_specs=[smem, _resident(bucket.shape), _resident(sink_rows.shape), per_seq(q.shape[1:]),
                  per_seq((pad, 2 * gd)), per_seq((win, gd)), per_seq((win, gd))],
        out_specs=[per_seq(q.shape[1:]), per_seq((win, gd)), per_seq((win, gd))],
        out_shape=[jax.ShapeDtypeStruct(q.shape, BF16), jax.ShapeDtypeStruct(cache_k.shape, F32),
                   jax.ShapeDtypeStruct(cache_v.shape, F32)],
        scratch_shapes=[pltpu.VMEM((rows, keys), F32), pltpu.VMEM((keys, gd), F32), pltpu.VMEM((keys, gd), F32)],
        compiler_params=_params("arbitrary"),
        name="attn_sample",
    )(table, bucket, sink_rows, q, kv_new, cache_k, cache_v)


SEQ_BLOCK = 8
SLOT = 16


def _to_slots(slabs, bb):
    c = slabs[0].shape[1]
    stack = jnp.concatenate(list(slabs) + [jnp.zeros(((SUBLANES - len(slabs)) * bb, c), F32)], axis=0)
    row = lax.broadcasted_iota(jnp.int32, (bb * SLOT, SUBLANES * bb), 0)
    col = lax.broadcasted_iota(jnp.int32, (bb * SLOT, SUBLANES * bb), 1)
    place = jnp.where((row // SLOT == col % bb) & (row % SLOT == col // bb), 1.0, 0.0).astype(BF16)
    return _dot(place, stack.astype(BF16))


def _ssd_sample_kernel(z_ref, xs_ref, b_ref, c_ref, dt_ref, h0_ref, alog_ref, dexp_ref, nw_ref,
                       y_ref, hn_ref, cl_scr, yo_scr, ltt_scr, rhs_scr, *, groups, rep, hd, n_state):
    n_pos, bb, d_inner = xs_ref.shape
    gw = rep * hd

    a = -jnp.exp(alog_ref[...])
    expander = _head_expander(groups * rep, hd)
    dt = [dt_ref[l] for l in range(n_pos)]
    cs, run = [], None
    for l in range(n_pos):
        run = dt[l] * a if run is None else run + dt[l] * a
        cs.append(run)
    csx = [_dot_exact_rhs(v, expander) for v in cs]
    dtx = [_dot_exact_rhs(v, expander) for v in dt]
    x = [xs_ref[l] for l in range(n_pos)]
    bm = [b_ref[l] for l in range(n_pos)]
    cm = [c_ref[l] for l in range(n_pos)]

    xw = [x[s] * (jnp.exp(csx[n_pos - 1] - csx[s]) * dtx[s]) for s in range(n_pos)]
    decay_parts = [part.astype(F32) for part in _split3(jnp.exp(csx[n_pos - 1]))]
    lt = _to_slots(xw + decay_parts, bb)
    rb = _to_slots(bm, bb)
    cl_scr[...] = _to_slots(cm, bb).astype(BF16)

    slot_of_row = lax.broadcasted_iota(jnp.int32, (bb * SLOT, n_state), 0) % SLOT
    ones_rows = jnp.where((slot_of_row >= n_pos) & (slot_of_row < n_pos + 3), 1.0, 0.0)
    for g in range(groups):
        ltt_scr[g] = lt[:, g * gw:(g + 1) * gw].T.astype(BF16)
        rhs_scr[g] = jnp.concatenate([rb[:, g * n_state:(g + 1) * n_state], ones_rows],
                                     axis=1).astype(BF16)

    seq_of_row = lax.broadcasted_iota(jnp.int32, (bb * SLOT, 2 * n_state), 0) // SLOT

    def per_sequence(i, carry):
        r0 = pl.multiple_of(i * SLOT, SLOT)
        mine = seq_of_row == i
        for g in range(groups):
            h0 = h0_ref[i, g]
            cq = cl_scr[pl.ds(r0, SLOT), g * n_state:(g + 1) * n_state]
            yo_scr[pl.ds(r0, SLOT), g * gw:(g + 1) * gw] = _dot_nt(cq, h0.astype(BF16))
            rhs = rhs_scr[g]
            sd = _dot(ltt_scr[g], jnp.where(mine, rhs, jnp.zeros_like(rhs)))
            hn_ref[i, g] = h0 * sd[:, n_state:] + sd[:, :n_state]
        return carry

    lax.fori_loop(0, bb, per_sequence, 0)

    yo_parts = _split3(yo_scr[...])
    pick_row = lax.broadcasted_iota(jnp.int32, (bb, bb * SLOT), 0)
    pick_col = lax.broadcasted_iota(jnp.int32, (bb, bb * SLOT), 1)
    for l in range(n_pos):
        pick = jnp.where(pick_col == pick_row * SLOT + l, 1.0, 0.0).astype(BF16)
        y_off = _dot(pick, yo_parts[0]) + _dot(pick, yo_parts[1]) + _dot(pick, yo_parts[2])
        acc = None
        for s in range(l + 1):
            prod = cm[l] * bm[s]
            cbx = jnp.concatenate(
                [jnp.broadcast_to(jnp.sum(prod[:, g * n_state:(g + 1) * n_state], axis=-1, keepdims=True),
                                  (bb, gw)) for g in range(groups)], axis=1)
            term = cbx * jnp.exp(csx[l] - csx[s]) * dtx[s] * x[s]
            acc = term if acc is None else acc + term
        y = acc + y_off * jnp.exp(csx[l])
        y = (y + dexp_ref[...] * x[l]) * _silu(z_ref[l])
        normed = []
        for g in range(groups):
            yg = y[:, g * gw:(g + 1) * gw]
            normed.append(yg * lax.rsqrt(jnp.mean(yg * yg, axis=-1, keepdims=True) + RMS_EPS))
        y_ref[l] = jnp.concatenate(normed, axis=1) * nw_ref[...]


def ssd_sample(z, xs, bm, cm, dt, h0, a_log, d_exp, norm_w):
    n_pos, bsz, d_inner = xs.shape
    groups, n_state, hd = SSM_GROUPS, D_STATE, SSM_HEAD_DIM
    rep = d_inner // (groups * hd)
    bb = SEQ_BLOCK
    kern = functools.partial(_ssd_sample_kernel, groups=groups, rep=rep, hd=hd, n_state=n_state)

    def slab(n):
        return pl.BlockSpec((n_pos, bb, n), lambda i: (0, i, 0))

    h_spec = pl.BlockSpec((bb,) + h0.shape[1:], lambda i: (i, 0, 0, 0))
    rows = bb * SLOT
    return pl.pallas_call(
        kern,
        grid=(bsz // bb,),
        in_specs=[slab(d_inner), slab(d_inner), slab(bm.shape[2]), slab(cm.shape[2]), slab(LANES), h_spec,
                  _resident((1, LANES)), _resident((1, d_inner)), _resident((1, d_inner))],
        out_specs=[slab(d_inner), h_spec],
        out_shape=[jax.ShapeDtypeStruct(xs.shape, F32), jax.ShapeDtypeStruct(h0.shape, F32)],
        scratch_shapes=[pltpu.VMEM((rows, groups * n_state), BF16), pltpu.VMEM((rows, d_inner), F32),
                        pltpu.VMEM((groups, rep * hd, rows), BF16),
                        pltpu.VMEM((groups, rows, 2 * n_state), BF16)],
        compiler_params=_params("arbitrary"),
        name="ssd_sample",
    )(z, xs, bm, cm, dt, h0, a_log, d_exp, norm_w)


def _pad_lanes(v):
    return jnp.pad(v.astype(F32), (0, LANES - v.shape[0])).reshape(1, LANES)


def prep_ssm(w_in, dt_bias, a_log, d_skip):
    n_heads = dt_bias.shape[0]
    d_inner = n_heads * SSM_HEAD_DIM
    conv_dim = w_in.shape[1] - d_inner - n_heads
    w = w_in.astype(BF16)
    return dict(
        d_inner=d_inner,
        w_z=w[:, :d_inner],
        w_xbc=w[:, d_inner:d_inner + conv_dim],
        w_dt=jnp.pad(w[:, d_inner + conv_dim:], ((0, 0), (0, LANES - n_heads))),
        dt_bias=_pad_lanes(dt_bias),
        a_log=_pad_lanes(a_log),
        d_exp=jnp.repeat(d_skip.astype(F32), SSM_HEAD_DIM).reshape(1, d_inner))


def attn_layer_prompt(x, g, wqkv, wo, sinks, table, *, n_seq):
    m = x.shape[0]
    nq = N_HEADS * HEAD_DIM
    nkv = N_KV_HEADS * HEAD_DIM
    q, kv = norm_proj(x, g, wqkv, [(nq, BF16), (2 * nkv, F32)])
    o = attn_prompt(q, kv, table, sinks, n_seq=n_seq)
    tail = kv.reshape(n_seq, m // n_seq, 2 * nkv)[:, -WINDOW:]
    k_win = tail[..., :nkv].reshape(n_seq, WINDOW, N_KV_HEADS, HEAD_DIM)
    v_win = tail[..., nkv:].reshape(n_seq, WINDOW, N_KV_HEADS, HEAD_DIM)
    return proj_residual(o, wo, x), k_win, v_win


def ssd_layer_prompt(x, g, ssm, conv_w, conv_b, norm_w, w_out, *, n_seq):
    d_inner = ssm["d_inner"]
    taps = conv_w.shape[0]
    z, xs, bm, cm, dt, conv_st = ssd_in_proj(x, g, ssm["w_z"], ssm["w_xbc"], ssm["w_dt"], conv_w, conv_b,
                                             ssm["dt_bias"], d_inner=d_inner, n_seq=n_seq, stride=1)
    y, h = ssd_prompt(z, xs, bm, cm, dt, ssm["a_log"], ssm["d_exp"], norm_w.reshape(1, d_inner), n_seq=n_seq)
    rep = d_inner // (SSM_GROUPS * SSM_HEAD_DIM)
    h = h.reshape(n_seq, SSM_GROUPS, D_STATE, rep, SSM_HEAD_DIM)
    h = jnp.transpose(h, (0, 1, 3, 4, 2)).reshape(n_seq, SSM_GROUPS * rep, SSM_HEAD_DIM, D_STATE)
    return proj_residual(y, w_out, x), conv_st[:, -(taps - 1):], h


def _to_position_major(state):
    bsz, k, c = state.shape
    return jnp.swapaxes(state, 0, 1).reshape(1, k * bsz, c)


def _from_position_major(state, bsz):
    c = state.shape[-1]
    return jnp.swapaxes(state.reshape(-1, bsz, c), 0, 1)


def attn_layer_sample(x, g, wqkv, wo, sinks, table, cache_k, cache_v, *, bsz):
    m = x.shape[0]
    n_new = m // bsz
    nq = N_HEADS * HEAD_DIM
    nkv = N_KV_HEADS * HEAD_DIM
    win = cache_k.shape[1]
    q, kv = norm_proj(x, g, wqkv, [(nq, BF16), (2 * nkv, F32)])
    q = q.reshape(n_new, bsz, N_KV_HEADS, KV_REP, HEAD_DIM)
    q = jnp.transpose(q, (1, 3, 0, 2, 4)).reshape(bsz, KV_REP * n_new, nkv)
    kv = jnp.swapaxes(kv.reshape(n_new, bsz, 2 * nkv), 0, 1)
    o, new_k, new_v = attn_sample(q, kv, cache_k.reshape(bsz, win, nkv), cache_v.reshape(bsz, win, nkv),
                                  table, sinks)
    o = o.reshape(bsz, KV_REP, n_new, N_KV_HEADS, HEAD_DIM)
    o = jnp.transpose(o, (2, 0, 3, 1, 4)).reshape(m, nq)
    return proj_residual(o, wo, x), new_k.reshape(cache_k.shape), new_v.reshape(cache_v.shape)


def ssd_layer_sample(x, g, ssm, conv_w, conv_b, norm_w, w_out, conv_state, ssm_state, *, bsz):
    m = x.shape[0]
    n_new = m // bsz
    d_inner = ssm["d_inner"]
    rep = d_inner // (SSM_GROUPS * SSM_HEAD_DIM)
    z, xs, bm, cm, dt, conv_st = ssd_in_proj(x, g, ssm["w_z"], ssm["w_xbc"], ssm["w_dt"], conv_w, conv_b,
                                             ssm["dt_bias"], d_inner=d_inner, n_seq=1, stride=bsz,
                                             state=_to_position_major(conv_state))

    def slabs(v):
        return v.astype(F32).reshape(n_new, bsz, v.shape[1])

    h0 = ssm_state.reshape(bsz, SSM_GROUPS, rep * SSM_HEAD_DIM, D_STATE)
    y, h = ssd_sample(slabs(z), slabs(xs), slabs(bm), slabs(cm), slabs(dt), h0, ssm["a_log"], ssm["d_exp"],
                      norm_w.reshape(1, d_inner))
    out = proj_residual(y.reshape(m, d_inner), w_out, x)
    return out, _from_position_major(conv_st, bsz), h.reshape(ssm_state.shape)


def kernel(x_prompt, x_sample, cache_k_win, cache_v_win, state_ssm_conv, state_ssm, state_ffn_conv, rel_bias_table, norm_mix, norm_ffn, norm_final, attn_wqkv, attn_wo, attn_sinks, ssm_w_in, ssm_conv_w, ssm_conv_b, ssm_dt_bias, ssm_A_log, ssm_D, ssm_norm, ssm_w_out, ffn_w_up, ffn_conv_w, ffn_conv_b, ffn_w_down):
    bsz, seq, d = x_prompt.shape
    wqkv, wo = attn_wqkv[0].astype(BF16), attn_wo[0].astype(BF16)
    w_up, w_down = ffn_w_up.astype(BF16), ffn_w_down.astype(BF16)
    ssm = prep_ssm(ssm_w_in[0], ssm_dt_bias[0], ssm_A_log[0], ssm_D[0])
    w_out = ssm_w_out[0].astype(BF16)
    ffn_taps = ffn_conv_w.shape[1]

    x = x_prompt.reshape(bsz * seq, d)
    x, p_k, p_v = attn_layer_prompt(x, norm_mix[0], wqkv, wo, attn_sinks[0], rel_bias_table, n_seq=bsz)
    x, p_ffn0 = conv_ffn(x, norm_ffn[0], w_up[0], ffn_conv_w[0], ffn_conv_b[0], w_down[0], n_seq=bsz, stride=1)
    x, p_conv, p_ssm = ssd_layer_prompt(x, norm_mix[1], ssm, ssm_conv_w[0], ssm_conv_b[0], ssm_norm[0], w_out,
                                        n_seq=bsz)
    y, p_ffn1 = conv_ffn(x, norm_ffn[1], w_up[1], ffn_conv_w[1], ffn_conv_b[1], w_down[1], n_seq=bsz, stride=1,
                         g_final=norm_final)
    p_ffn = jnp.stack([p_ffn0, p_ffn1])[:, :, -(ffn_taps - 1):]

    dbs, dseq, _ = x_sample.shape
    xs = jnp.swapaxes(x_sample, 0, 1).reshape(dseq * dbs, d)
    xs, s_k, s_v = attn_layer_sample(xs, norm_mix[0], wqkv, wo, attn_sinks[0], rel_bias_table,
                                     cache_k_win[0], cache_v_win[0], bsz=dbs)
    xs, s_ffn0 = conv_ffn(xs, norm_ffn[0], w_up[0], ffn_conv_w[0], ffn_conv_b[0], w_down[0], n_seq=1, stride=dbs,
                          state=_to_position_major(state_ffn_conv[0]))
    xs, s_conv, s_ssm = ssd_layer_sample(xs, norm_mix[1], ssm, ssm_conv_w[0], ssm_conv_b[0], ssm_norm[0], w_out,
                                         state_ssm_conv[0], state_ssm[0], bsz=dbs)
    ys, s_ffn1 = conv_ffn(xs, norm_ffn[1], w_up[1], ffn_conv_w[1], ffn_conv_b[1], w_down[1], n_seq=1, stride=dbs,
                          state=_to_position_major(state_ffn_conv[1]), g_final=norm_final)
    ys = jnp.swapaxes(ys.reshape(dseq, dbs, d), 0, 1)
    s_ffn = jnp.stack([_from_position_major(s_ffn0, dbs), _from_position_major(s_ffn1, dbs)])
    return (y.reshape(bsz, seq, d), ys, p_k[None], p_v[None], p_conv[None], p_ssm[None], p_ffn,
            s_k[None], s_v[None], s_conv[None], s_ssm[None], s_ffn)
```

```python
import functools
import math

import jax
import jax.numpy as jnp
from jax import lax
from jax.experimental import pallas as pl
from jax.experimental.pallas import tpu as pltpu

F32 = jnp.float32
BF16 = jnp.bfloat16

RMS_EPS = 1e-6
N_HEADS = 16
N_KV_HEADS = 4
HEAD_DIM = 64
KV_REP = N_HEADS // N_KV_HEADS
WINDOW = 128
NUM_BUCKETS = 32
MAX_DISTANCE = 128
SSM_HEAD_DIM = 64
SSM_GROUPS = 8
D_STATE = 128
SSD_CHUNK = 128

VMEM_LIMIT_BYTES = 56 * 1024 * 1024
SUBLANES = 8
LANES = 128
ROW_TILE = 512


def _params(*semantics):
    return pltpu.CompilerParams(dimension_semantics=semantics, vmem_limit_bytes=VMEM_LIMIT_BYTES)


def _resident(shape):
    return pl.BlockSpec(shape, lambda *_: (0,) * len(shape), pipeline_mode=pl.Buffered(1))


def _rmsnorm(x, g):
    return x * lax.rsqrt(jnp.mean(x * x, axis=-1, keepdims=True) + RMS_EPS) * g


def _silu(x):
    return x * jax.nn.sigmoid(x)


def _dot(a, b):
    return jnp.dot(a, b, preferred_element_type=F32)


def _dot_nt(a, b):
    return lax.dot_general(a, b, (((1,), (1,)), ((), ())), preferred_element_type=F32)


def _dot_tn(a, b):
    return lax.dot_general(a, b, (((0,), (0,)), ((), ())), preferred_element_type=F32)


def _split3(x):
    hi = x.astype(BF16)
    r1 = x - hi.astype(F32)
    mid = r1.astype(BF16)
    lo = (r1 - mid.astype(F32)).astype(BF16)
    return hi, mid, lo


def _dot_exact_rhs(x, sel):
    hi, mid, lo = _split3(x)
    return _dot(hi, sel) + _dot(mid, sel) + _dot(lo, sel)


def _dot_exact_lhs(sel, x):
    hi, mid, lo = _split3(x)
    return _dot(sel, hi) + _dot(sel, mid) + _dot(sel, lo)


def _norm_proj_kernel(x_ref, g_ref, w_ref, *out_refs):
    h = _rmsnorm(x_ref[...], g_ref[...]).astype(BF16)
    c0 = 0
    for o_ref in out_refs:
        c1 = c0 + o_ref.shape[1]
        o_ref[...] = _dot(h, w_ref[:, c0:c1]).astype(o_ref.dtype)
        c0 = c1


def norm_proj(x, g, w, out_cols_dtypes):
    m, d = x.shape
    tm = min(ROW_TILE, m)
    return pl.pallas_call(
        _norm_proj_kernel,
        grid=(m // tm,),
        in_specs=[pl.BlockSpec((tm, d), lambda i: (i, 0)), _resident((1, d)), _resident(w.shape)],
        out_specs=[pl.BlockSpec((tm, n), lambda i: (i, 0)) for n, _ in out_cols_dtypes],
        out_shape=[jax.ShapeDtypeStruct((m, n), dt) for n, dt in out_cols_dtypes],
        compiler_params=_params("arbitrary"),
        name="norm_proj",
    )(x, g.reshape(1, d), w)


def _proj_residual_kernel(a_ref, w_ref, x_ref, o_ref):
    o_ref[...] = x_ref[...] + _dot(a_ref[...].astype(BF16), w_ref[...])


def proj_residual(a, w, x):
    m, k = a.shape
    d = w.shape[1]
    tm = min(ROW_TILE, m)
    return pl.pallas_call(
        _proj_residual_kernel,
        grid=(m // tm,),
        in_specs=[pl.BlockSpec((tm, k), lambda i: (i, 0)), _resident(w.shape),
                  pl.BlockSpec((tm, d), lambda i: (i, 0))],
        out_specs=pl.BlockSpec((tm, d), lambda i: (i, 0)),
        out_shape=jax.ShapeDtypeStruct((m, d), F32),
        compiler_params=_params("arbitrary"),
        name="proj_residual",
    )(a, w, x)


def _conv_taps(ubuf, cols, cw_ref, cb_ref, c0, width, *, tm, halo, stride, taps):
    y = cb_ref[:, c0:c0 + width]
    for k in range(taps):
        off = halo - (taps - 1 - k) * stride
        y = y + ubuf[off:off + tm, cols] * cw_ref[k:k + 1, c0:c0 + width]
    return y


def _conv_ffn_kernel(*refs, tm, halo, stride, taps, tf, d_ff, has_state, final_norm):
    it = iter(refs)
    x_ref, g_ref, wup_ref, cw_ref, cb_ref, wdn_ref = (next(it) for _ in range(6))
    init_ref = next(it) if has_state else None
    gfin_ref = next(it) if final_norm else None
    o_ref, st_ref, ubufs = next(it), next(it), next(it)

    @pl.when(pl.program_id(1) == 0)
    def _():
        st_ref[0] = init_ref[0] if has_state else jnp.zeros(st_ref.shape[1:], F32)

    x = x_ref[...]
    h = _rmsnorm(x, g_ref[...]).astype(BF16)
    n_chunks = d_ff // tf

    def up_proj(j):
        ubuf = ubufs.at[j % 2]
        for half in range(2):
            c0 = half * d_ff + j * tf
            cols = slice(half * tf, (half + 1) * tf)
            ubuf[0:halo, cols] = st_ref[0, :, c0:c0 + tf]
            ubuf[halo:halo + tm, cols] = _dot(h, wup_ref[:, c0:c0 + tf])

    up_proj(0)
    acc = None
    for j in range(n_chunks):
        if j + 1 < n_chunks:
            up_proj(j + 1)
        ubuf = ubufs.at[j % 2]
        conv = []
        for half in range(2):
            c0 = half * d_ff + j * tf
            cols = slice(half * tf, (half + 1) * tf)
            conv.append(_conv_taps(ubuf, cols, cw_ref, cb_ref, c0, tf,
                                   tm=tm, halo=halo, stride=stride, taps=taps))
            st_ref[0, :, c0:c0 + tf] = ubuf[tm:tm + halo, cols]
        a = (_silu(conv[0]) * conv[1]).astype(BF16)
        part = _dot(a, wdn_ref[j * tf:(j + 1) * tf, :])
        acc = part if acc is None else acc + part
    out = x + acc
    if final_norm:
        out = _rmsnorm(out, gfin_ref[...])
    o_ref[...] = out


def conv_ffn(x, g, w_up, conv_w, conv_b, w_down, *, n_seq, stride, state=None, g_final=None):
    m, d = x.shape
    d_ff = w_down.shape[0]
    taps = conv_w.shape[0]
    rows = m // n_seq
    tm = min(ROW_TILE, rows)
    halo = max(SUBLANES, (taps - 1) * stride)
    tf = 256
    kern = functools.partial(_conv_ffn_kernel, tm=tm, halo=halo, stride=stride, taps=taps, tf=tf,
                             d_ff=d_ff, has_state=state is not None, final_norm=g_final is not None)
    tiles = rows // tm
    row_spec = pl.BlockSpec((tm, d), lambda s, t: (s * tiles + t, 0))
    st_spec = pl.BlockSpec((1, halo, 2 * d_ff), lambda s, t: (s, 0, 0))
    in_specs = [row_spec, _resident((1, d)), _resident(w_up.shape), _resident(conv_w.shape),
                _resident((1, 2 * d_ff)), _resident(w_down.shape)]
    args = [x, g.reshape(1, d), w_up, conv_w, conv_b.reshape(1, 2 * d_ff), w_down]
    if state is not None:
        in_specs.append(st_spec)
        args.append(state)
    if g_final is not None:
        in_specs.append(_resident((1, d)))
        args.append(g_final.reshape(1, d))
    return pl.pallas_call(
        kern,
        grid=(n_seq, tiles),
        in_specs=in_specs,
        out_specs=[row_spec, st_spec],
        out_shape=[jax.ShapeDtypeStruct((m, d), F32),
                   jax.ShapeDtypeStruct((n_seq, halo, 2 * d_ff), F32)],
        scratch_shapes=[pltpu.VMEM((2, halo + tm, 2 * tf), F32)],
        compiler_params=_params("arbitrary", "arbitrary"),
        name="conv_ffn",
    )(*args)


def _softplus(x):
    return jnp.maximum(x, 0.0) + jnp.log1p(jnp.exp(-jnp.abs(x)))


def _ssd_in_proj_kernel(*refs, tm, halo, stride, taps, tc, has_state):
    it = iter(refs)
    x_ref, g_ref, wz_ref, wx_ref, wdt_ref, cw_ref, cb_ref, dtb_ref = (next(it) for _ in range(8))
    init_ref = next(it) if has_state else None
    z_ref, xs_ref, b_ref, c_ref, dt_ref, st_ref, ubufs = (next(it) for _ in range(7))

    @pl.when(pl.program_id(1) == 0)
    def _():
        st_ref[0] = init_ref[0] if has_state else jnp.zeros(st_ref.shape[1:], F32)

    h = _rmsnorm(x_ref[...], g_ref[...]).astype(BF16)
    n_chunks = wx_ref.shape[1] // tc

    def project(j):
        ubuf = ubufs.at[j % 2]
        ubuf[0:halo, :] = st_ref[0, :, j * tc:(j + 1) * tc]
        ubuf[halo:halo + tm, :] = _dot(h, wx_ref[:, j * tc:(j + 1) * tc])

    project(0)
    z_ref[...] = _dot(h, wz_ref[...])
    dt_ref[...] = _softplus(_dot(h, wdt_ref[...]) + dtb_ref[...])
    outs = ((xs_ref, 0), (b_ref, xs_ref.shape[1]), (c_ref, xs_ref.shape[1] + b_ref.shape[1]))
    for j in range(n_chunks):
        if j + 1 < n_chunks:
            project(j + 1)
        ubuf = ubufs.at[j % 2]
        c0 = j * tc
        y = _silu(_conv_taps(ubuf, slice(None), cw_ref, cb_ref, c0, tc,
                             tm=tm, halo=halo, stride=stride, taps=taps))
        st_ref[0, :, c0:c0 + tc] = ubuf[tm:tm + halo, :]
        o_ref, base = [(r, b0) for r, b0 in outs if b0 <= c0][-1]
        o_ref[:, c0 - base:c0 - base + tc] = y.astype(o_ref.dtype)


def ssd_in_proj(x, g, w_z, w_xbc, w_dt, conv_w, conv_b, dt_bias, *, d_inner, n_seq, stride, state=None):
    m, d = x.shape
    conv_dim = w_xbc.shape[1]
    gn = (conv_dim - d_inner) // 2
    taps = conv_w.shape[0]
    rows = m // n_seq
    tm = min(ROW_TILE, rows)
    halo = max(SUBLANES, (taps - 1) * stride)
    tc = 512
    kern = functools.partial(_ssd_in_proj_kernel, tm=tm, halo=halo, stride=stride, taps=taps, tc=tc,
                             has_state=state is not None)
    tiles = rows // tm

    def rows_of(n):
        return pl.BlockSpec((tm, n), lambda s, t: (s * tiles + t, 0))

    st_spec = pl.BlockSpec((1, halo, conv_dim), lambda s, t: (s, 0, 0))
    in_specs = [rows_of(d), _resident((1, d)), _resident(w_z.shape), _resident(w_xbc.shape),
                _resident(w_dt.shape), _resident(conv_w.shape), _resident((1, conv_dim)),
                _resident((1, LANES))]
    args = [x, g.reshape(1, d), w_z, w_xbc, w_dt, conv_w, conv_b.reshape(1, conv_dim), dt_bias]
    if state is not None:
        in_specs.append(st_spec)
        args.append(state)
    return pl.pallas_call(
        kern,
        grid=(n_seq, tiles),
        in_specs=in_specs,
        out_specs=[rows_of(d_inner), rows_of(d_inner), rows_of(gn), rows_of(gn), rows_of(LANES), st_spec],
        out_shape=[jax.ShapeDtypeStruct((m, d_inner), F32), jax.ShapeDtypeStruct((m, d_inner), F32),
                   jax.ShapeDtypeStruct((m, gn), BF16), jax.ShapeDtypeStruct((m, gn), BF16),
                   jax.ShapeDtypeStruct((m, LANES), F32),
                   jax.ShapeDtypeStruct((n_seq, halo, conv_dim), F32)],
        scratch_shapes=[pltpu.VMEM((2, halo + tm, tc), F32)],
        compiler_params=_params("arbitrary", "arbitrary"),
        name="ssd_in_proj",
    )(*args)


def _block_diag_rows(a, rep, width):
    col_block = lax.broadcasted_iota(jnp.int32, a.shape, 1) // width
    zero = jnp.zeros_like(a)
    return jnp.concatenate([jnp.where(col_block == r, a, zero) for r in range(rep)], axis=0)


def _head_expander(n_heads, width):
    row = lax.broadcasted_iota(jnp.int32, (LANES, n_heads * width), 0)
    col = lax.broadcasted_iota(jnp.int32, (LANES, n_heads * width), 1)
    return jnp.where(row == col // width, 1.0, 0.0).astype(BF16)


def _ssd_prompt_kernel(z_ref, xs_ref, b_ref, c_ref, dt_ref, alog_ref, dexp_ref, nw_ref,
                       y_ref, h_ref, *, groups, rep, hd, n_state):
    t = dt_ref.shape[0]
    gw = rep * hd

    @pl.when(pl.program_id(1) == 0)
    def _():
        h_ref[...] = jnp.zeros(h_ref.shape, F32)

    dt = dt_ref[...]
    da = dt * (-jnp.exp(alog_ref[...]))
    row = lax.broadcasted_iota(jnp.int32, (t, t), 0)
    col = lax.broadcasted_iota(jnp.int32, (t, t), 1)
    causal = row >= col
    cs = _dot_exact_lhs(jnp.where(causal, 1.0, 0.0).astype(BF16), da)
    cs_t = cs.T
    dt_t = dt.T
    dd_t = jnp.exp(cs_t[:, t - 1:t] - cs_t) * dt_t
    chunk_decay = jnp.exp(jnp.broadcast_to(cs[t - 1:t, :], (SUBLANES, LANES)))
    cdx = _dot_exact_rhs(chunk_decay, _head_expander(groups * rep, hd))[0:1, :]

    for g in range(groups):
        gs = slice(g * gw, (g + 1) * gw)
        ns = slice(g * n_state, (g + 1) * n_state)
        bg = b_ref[:, ns]
        cg = c_ref[:, ns]
        cb = _dot_nt(cg, bg)
        b_t = bg.astype(F32).T
        cg32 = cg.astype(F32)
        xg = xs_ref[:, gs]
        x_bd = _block_diag_rows(xg.astype(BF16), rep, hd)
        hg = h_ref[0, g]
        h_bd = _block_diag_rows(hg.astype(BF16), rep, hd)
        w_parts, c_parts, s_parts = [], [], []
        for r in range(rep):
            hh = g * rep + r
            cs_col = cs[:, hh:hh + 1]
            seg = cs_col - cs_t[hh:hh + 1, :]
            lmat = jnp.exp(jnp.where(causal, seg, -jnp.inf))
            w_parts.append((cb * lmat * dt_t[hh:hh + 1, :]).astype(BF16))
            c_parts.append((cg32 * jnp.exp(cs_col)).astype(BF16))
            s_parts.append((b_t * dd_t[hh:hh + 1, :]).astype(BF16))
        y = _dot(jnp.concatenate(w_parts + c_parts, axis=1), jnp.concatenate([x_bd, h_bd], axis=0))
        h_ref[0, g] = hg * cdx[:, gs] + _dot(jnp.concatenate(s_parts, axis=1), x_bd)
        y = (y + dexp_ref[:, gs] * xg) * _silu(z_ref[:, gs])
        y = y * lax.rsqrt(jnp.mean(y * y, axis=-1, keepdims=True) + RMS_EPS) * nw_ref[:, gs]
        y_ref[:, gs] = y.astype(y_ref.dtype)


def ssd_prompt(z, xs, bm, cm, dt, a_log, d_exp, norm_w, *, n_seq):
    m, d_inner = xs.shape
    gn = bm.shape[1]
    groups, n_state, hd = SSM_GROUPS, D_STATE, SSM_HEAD_DIM
    rep = d_inner // (groups * hd)
    t = SSD_CHUNK
    chunks = m // n_seq // t
    kern = functools.partial(_ssd_prompt_kernel, groups=groups, rep=rep, hd=hd, n_state=n_state)

    def rows_of(n):
        return pl.BlockSpec((t, n), lambda s, c: (s * chunks + c, 0))

    h_spec = pl.BlockSpec((1, groups, n_state, rep * hd), lambda s, c: (s, 0, 0, 0))
    return pl.pallas_call(
        kern,
        grid=(n_seq, chunks),
        in_specs=[rows_of(d_inner), rows_of(d_inner), rows_of(gn), rows_of(gn), rows_of(LANES),
                  _resident((1, LANES)), _resident((1, d_inner)), _resident((1, d_inner))],
        out_specs=[rows_of(d_inner), h_spec],
        out_shape=[jax.ShapeDtypeStruct((m, d_inner), BF16),
                   jax.ShapeDtypeStruct((n_seq, groups, n_state, rep * hd), F32)],
        compiler_params=_params("arbitrary", "arbitrary"),
        name="ssd_prompt",
    )(z, xs, bm, cm, dt, a_log, d_exp, norm_w)


def _bias_from_buckets(bucket, table_ref, head):
    acc = jnp.zeros(bucket.shape, F32)
    for j in range(NUM_BUCKETS):
        acc = jnp.where(bucket == j, table_ref[j, head], acc)
    return jnp.where(bucket >= 0, acc, -jnp.inf)


def _sink_softmax(s, sink):
    m = jnp.maximum(jnp.max(s, axis=-1, keepdims=True), sink)
    p = jnp.exp(s - m)
    return p / (jnp.sum(p, axis=-1, keepdims=True) + jnp.exp(sink - m))


def _attn_prompt_kernel(table_ref, sinks_ref, bucket_ref, q_ref, kvp_ref, kvo_ref, o_ref, bias_ref):
    blk = q_ref.shape[0]
    gd = N_KV_HEADS * HEAD_DIM
    first_block = pl.program_id(1) == 0

    @pl.when((pl.program_id(0) == 0) & first_block)
    def _():
        bucket = bucket_ref[...]
        own_key = lax.broadcasted_iota(jnp.int32, bucket.shape, 0) >= blk
        for hh in range(N_HEADS):
            g, r = divmod(hh, KV_REP)
            bias = _bias_from_buckets(bucket, table_ref, hh)
            bias_ref[0, g, :, r * blk:(r + 1) * blk] = bias
            bias_ref[1, g, :, r * blk:(r + 1) * blk] = jnp.where(own_key, bias, -jnp.inf)

    kv = jnp.concatenate([kvp_ref[...], kvo_ref[...]], axis=0)
    k16 = kv[:, :gd].astype(BF16)
    v_t = kv[:, gd:].T.astype(BF16)
    which_bias = jnp.where(first_block, 1, 0)
    lane_group = lax.broadcasted_iota(jnp.int32, (blk, gd), 1) // HEAD_DIM
    q = q_ref[...] * jnp.asarray(HEAD_DIM ** -0.5, q_ref.dtype)
    groups = range(N_KV_HEADS)

    scores, sinks = [], []
    for g in groups:
        q_g = jnp.concatenate(
            [jnp.where(lane_group == g, q[:, r * gd:(r + 1) * gd], jnp.zeros((), q.dtype))
             for r in range(KV_REP)], axis=0)
        scores.append(_dot_nt(k16, q_g) + bias_ref[which_bias, g])
        sinks.append(jnp.concatenate([jnp.full((1, blk), sinks_ref[g * KV_REP + r], F32)
                                      for r in range(KV_REP)], axis=1))
    maxes = [jnp.maximum(jnp.max(scores[g], axis=0, keepdims=True), sinks[g]) for g in groups]
    probs = [jnp.exp(scores[g] - maxes[g]) for g in groups]
    denoms = [jnp.sum(probs[g], axis=0, keepdims=True) + jnp.exp(sinks[g] - maxes[g]) for g in groups]
    zeros = jnp.zeros((HEAD_DIM, 2 * blk), BF16)
    for g in groups:
        p = (probs[g] * (1.0 / denoms[g])).astype(BF16)
        p_rows = jnp.concatenate([p[:, r * blk:(r + 1) * blk] for r in range(KV_REP)], axis=0)
        v_g = v_t[g * HEAD_DIM:(g + 1) * HEAD_DIM, :]
        v_bd = jnp.concatenate(
            [jnp.concatenate([v_g if rr == r else zeros for rr in range(KV_REP)], axis=1)
             for r in range(KV_REP)], axis=0)
        o_t = _dot(v_bd, p_rows)
        o_ref[:, g * KV_REP * HEAD_DIM:(g + 1) * KV_REP * HEAD_DIM] = o_t.T.astype(o_ref.dtype)


def _t5_bucket(dist):
    max_exact = NUM_BUCKETS // 2
    nf = jnp.maximum(dist, 1).astype(F32)
    large = max_exact + (jnp.log(nf / max_exact) / math.log(MAX_DISTANCE / max_exact)
                         * (NUM_BUCKETS - max_exact)).astype(jnp.int32)
    large = jnp.minimum(large, NUM_BUCKETS - 1)
    return jnp.where(dist < max_exact, dist, large)


def attn_prompt(q, kv, table, sinks, *, n_seq):
    m = q.shape[0]
    blk = WINDOW
    nb = m // n_seq // blk
    dist = (jnp.arange(blk)[None, :] + blk) - jnp.arange(2 * blk)[:, None]
    bucket = jnp.where((dist >= 0) & (dist < WINDOW), _t5_bucket(jnp.maximum(dist, 0)), -1).astype(jnp.int32)
    smem = pl.BlockSpec(memory_space=pltpu.SMEM)
    return pl.pallas_call(
        _attn_prompt_kernel,
        grid=(n_seq, nb),
        in_specs=[smem, smem, _resident(bucket.shape),
                  pl.BlockSpec((blk, q.shape[1]), lambda s, c: (s * nb + c, 0)),
                  pl.BlockSpec((blk, kv.shape[1]), lambda s, c: (s * nb + jnp.maximum(c - 1, 0), 0)),
                  pl.BlockSpec((blk, kv.shape[1]), lambda s, c: (s * nb + c, 0))],
        out_specs=pl.BlockSpec((blk, q.shape[1]), lambda s, c: (s * nb + c, 0)),
        out_shape=jax.ShapeDtypeStruct(q.shape, BF16),
        scratch_shapes=[pltpu.VMEM((2, N_KV_HEADS, 2 * blk, KV_REP * blk), F32)],
        compiler_params=_params("arbitrary", "arbitrary"),
        name="attn_prompt",
    )(table, sinks, bucket, q, kv, kv)


def _attn_sample_kernel(table_ref, bucket_ref, sink_ref, q_ref, kvn_ref, ck_ref, cv_ref,
                        o_ref, nk_ref, nv_ref, bias_ref, kall, vall, *, n_new):
    bb, win, gd = ck_ref.shape
    rows = bias_ref.shape[0]
    per_group = rows // N_KV_HEADS
    pad = kvn_ref.shape[1]

    @pl.when(pl.program_id(0) == 0)
    def _():
        for hh in range(N_HEADS):
            rs = slice(hh * n_new, (hh + 1) * n_new)
            bias_ref[rs, :] = _bias_from_buckets(bucket_ref[rs, :], table_ref, hh)
        kall[win + pad:, :] = jnp.zeros((kall.shape[0] - win - pad, gd), F32)
        vall[win + pad:, :] = jnp.zeros((vall.shape[0] - win - pad, gd), F32)

    row_group = lax.broadcasted_iota(jnp.int32, (rows, gd), 0) // per_group
    col_group = lax.broadcasted_iota(jnp.int32, (rows, gd), 1) // HEAD_DIM
    own = row_group == col_group
    scale = HEAD_DIM ** -0.5
    for i in range(bb):
        kall[0:win, :] = ck_ref[i]
        kall[win:win + pad, :] = kvn_ref[i, :, 0:gd]
        vall[0:win, :] = cv_ref[i]
        vall[win:win + pad, :] = kvn_ref[i, :, gd:2 * gd]
        nk_ref[i] = kall[n_new:n_new + win, :]
        nv_ref[i] = vall[n_new:n_new + win, :]
        qi = q_ref[i]
        q_bd = jnp.where(own, jnp.concatenate([qi] * N_KV_HEADS, axis=0), jnp.zeros((), qi.dtype))
        s = _dot_nt(q_bd, kall[...].astype(BF16)) * scale + bias_ref[...]
        p = _sink_softmax(s, sink_ref[...]).astype(BF16)
        o = jnp.where(own, _dot(p, vall[...].astype(BF16)), 0.0)
        o_ref[i] = sum(o[g * per_group:(g + 1) * per_group] for g in range(N_KV_HEADS)).astype(o_ref.dtype)


def attn_sample(q, kv_new, cache_k, cache_v, table, sinks):
    bsz, win, gd = cache_k.shape
    n_new = kv_new.shape[1]
    rows = N_HEADS * n_new
    keys = 2 * LANES
    pad = SUBLANES
    bb = 8
    kv_new = jnp.pad(kv_new, ((0, 0), (0, pad - n_new), (0, 0)))
    dist = (win + jnp.arange(n_new))[:, None] - jnp.arange(keys)[None, :]
    ok = (dist >= 0) & (dist < WINDOW) & (jnp.arange(keys)[None, :] < win + n_new)
    bucket = jnp.where(ok, _t5_bucket(jnp.maximum(dist, 0)), -1).astype(jnp.int32)
    bucket = jnp.tile(bucket, (N_HEADS, 1))
    sink_rows = jnp.repeat(sinks.astype(F32), n_new).reshape(rows, 1)
    smem = pl.BlockSpec(memory_space=pltpu.SMEM)

    def per_seq(shape):
        return pl.BlockSpec((bb,) + shape, lambda i: (i, 0, 0))

    return pl.pallas_call(
        functools.partial(_attn_sample_kernel, n_new=n_new),
        grid=(bsz // bb,),
        in_specs=[smem, _resident(bucket.shape), _resident(sink_rows.shape), per_seq(q.shape[1:]),
                  per_seq((pad, 2 * gd)), per_seq((win, gd)), per_seq((win, gd))],
        out_specs=[per_seq(q.shape[1:]), per_seq((win, gd)), per_seq((win, gd))],
        out_shape=[jax.ShapeDtypeStruct(q.shape, BF16), jax.ShapeDtypeStruct(cache_k.shape, F32),
                   jax.ShapeDtypeStruct(cache_v.shape, F32)],
        scratch_shapes=[pltpu.VMEM((rows, keys), F32), pltpu.VMEM((keys, gd), F32), pltpu.VMEM((keys, gd), F32)],
        compiler_params=_params("arbitrary"),
        name="attn_sample",
    )(table, bucket, sink_rows, q, kv_new, cache_k, cache_v)


SEQ_BLOCK = 8
SLOT = 16


def _to_slots(slabs, bb):
    c = slabs[0].shape[1]
    stack = jnp.concatenate(list(slabs) + [jnp.zeros(((SUBLANES - len(slabs)) * bb, c), F32)], axis=0)
    row = lax.broadcasted_iota(jnp.int32, (bb * SLOT, SUBLANES * bb), 0)
    col = lax.broadcasted_iota(jnp.int32, (bb * SLOT, SUBLANES * bb), 1)
    place = jnp.where((row // SLOT == col % bb) & (row % SLOT == col // bb), 1.0, 0.0).astype(BF16)
    return _dot(place, stack.astype(BF16))


def _ssd_sample_kernel(z_ref, xs_ref, b_ref, c_ref, dt_ref, h0_ref, alog_ref, dexp_ref, nw_ref,
                       y_ref, hn_ref, cl_scr, yo_scr, ltt_scr, rhs_scr, *, groups, rep, hd, n_state):
    n_pos, bb, d_inner = xs_ref.shape
    gw = rep * hd

    a = -jnp.exp(alog_ref[...])
    expander = _head_expander(groups * rep, hd)
    dt = [dt_ref[l] for l in range(n_pos)]
    cs, run = [], None
    for l in range(n_pos):
        run = dt[l] * a if run is None else run + dt[l] * a
        cs.append(run)
    csx = [_dot_exact_rhs(v, expander) for v in cs]
    dtx = [_dot_exact_rhs(v, expander) for v in dt]
    x = [xs_ref[l] for l in range(n_pos)]
    bm = [b_ref[l] for l in range(n_pos)]
    cm = [c_ref[l] for l in range(n_pos)]

    xw = [x[s] * (jnp.exp(csx[n_pos - 1] - csx[s]) * dtx[s]) for s in range(n_pos)]
    decay_parts = [part.astype(F32) for part in _split3(jnp.exp(csx[n_pos - 1]))]
    lt = _to_slots(xw + decay_parts, bb)
    rb = _to_slots(bm, bb)
    cl_scr[...] = _to_slots(cm, bb).astype(BF16)

    slot_of_row = lax.broadcasted_iota(jnp.int32, (bb * SLOT, n_state), 0) % SLOT
    ones_rows = jnp.where((slot_of_row >= n_pos) & (slot_of_row < n_pos + 3), 1.0, 0.0)
    for g in range(groups):
        ltt_scr[g] = lt[:, g * gw:(g + 1) * gw].T.astype(BF16)
        rhs_scr[g] = jnp.concatenate([rb[:, g * n_state:(g + 1) * n_state], ones_rows],
                                     axis=1).astype(BF16)

    seq_of_row = lax.broadcasted_iota(jnp.int32, (bb * SLOT, 2 * n_state), 0) // SLOT

    def per_sequence(i, carry):
        r0 = pl.multiple_of(i * SLOT, SLOT)
        mine = seq_of_row == i
        for g in range(groups):
            h0 = h0_ref[i, g]
            cq = cl_scr[pl.ds(r0, SLOT), g * n_state:(g + 1) * n_state]
            yo_scr[pl.ds(r0, SLOT), g * gw:(g + 1) * gw] = _dot_nt(cq, h0.astype(BF16))
            rhs = rhs_scr[g]
            sd = _dot(ltt_scr[g], jnp.where(mine, rhs, jnp.zeros_like(rhs)))
            hn_ref[i, g] = h0 * sd[:, n_state:] + sd[:, :n_state]
        return carry

    lax.fori_loop(0, bb, per_sequence, 0)

    yo_parts = _split3(yo_scr[...])
    pick_row = lax.broadcasted_iota(jnp.int32, (bb, bb * SLOT), 0)
    pick_col = lax.broadcasted_iota(jnp.int32, (bb, bb * SLOT), 1)
    for l in range(n_pos):
        pick = jnp.where(pick_col == pick_row * SLOT + l, 1.0, 0.0).astype(BF16)
        y_off = _dot(pick, yo_parts[0]) + _dot(pick, yo_parts[1]) + _dot(pick, yo_parts[2])
        acc = None
        for s in range(l + 1):
            prod = cm[l] * bm[s]
            cbx = jnp.concatenate(
                [jnp.broadcast_to(jnp.sum(prod[:, g * n_state:(g + 1) * n_state], axis=-1, keepdims=True),
                                  (bb, gw)) for g in range(groups)], axis=1)
            term = cbx * jnp.exp(csx[l] - csx[s]) * dtx[s] * x[s]
            acc = term if acc is None else acc + term
        y = acc + y_off * jnp.exp(csx[l])
        y = (y + dexp_ref[...] * x[l]) * _silu(z_ref[l])
        normed = []
        for g in range(groups):
            yg = y[:, g * gw:(g + 1) * gw]
            normed.append(yg * lax.rsqrt(jnp.mean(yg * yg, axis=-1, keepdims=True) + RMS_EPS))
        y_ref[l] = jnp.concatenate(normed, axis=1) * nw_ref[...]


def ssd_sample(z, xs, bm, cm, dt, h0, a_log, d_exp, norm_w):
    n_pos, bsz, d_inner = xs.shape
    groups, n_state, hd = SSM_GROUPS, D_STATE, SSM_HEAD_DIM
    rep = d_inner // (groups * hd)
    bb = SEQ_BLOCK
    kern = functools.partial(_ssd_sample_kernel, groups=groups, rep=rep, hd=hd, n_state=n_state)

    def slab(n):
        return pl.BlockSpec((n_pos, bb, n), lambda i: (0, i, 0))

    h_spec = pl.BlockSpec((bb,) + h0.shape[1:], lambda i: (i, 0, 0, 0))
    rows = bb * SLOT
    return pl.pallas_call(
        kern,
        grid=(bsz // bb,),
        in_specs=[slab(d_inner), slab(d_inner), slab(bm.shape[2]), slab(cm.shape[2]), slab(LANES), h_spec,
                  _resident((1, LANES)), _resident((1, d_inner)), _resident((1, d_inner))],
        out_specs=[slab(d_inner), h_spec],
        out_shape=[jax.ShapeDtypeStruct(xs.shape, F32), jax.ShapeDtypeStruct(h0.shape, F32)],
        scratch_shapes=[pltpu.VMEM((rows, groups * n_state), BF16), pltpu.VMEM((rows, d_inner), F32),
                        pltpu.VMEM((groups, rep * hd, rows), BF16),
                        pltpu.VMEM((groups, rows, 2 * n_state), BF16)],
        compiler_params=_params("arbitrary"),
        name="ssd_sample",
    )(z, xs, bm, cm, dt, h0, a_log, d_exp, norm_w)


def _pad_lanes(v):
    return jnp.pad(v.astype(F32), (0, LANES - v.shape[0])).reshape(1, LANES)


def prep_ssm(w_in, dt_bias, a_log, d_skip):
    n_heads = dt_bias.shape[0]
    d_inner = n_heads * SSM_HEAD_DIM
    conv_dim = w_in.shape[1] - d_inner - n_heads
    w = w_in.astype(BF16)
    return dict(
        d_inner=d_inner,
        w_z=w[:, :d_inner],
        w_xbc=w[:, d_inner:d_inner + conv_dim],
        w_dt=jnp.pad(w[:, d_inner + conv_dim:], ((0, 0), (0, LANES - n_heads))),
        dt_bias=_pad_lanes(dt_bias),
        a_log=_pad_lanes(a_log),
        d_exp=jnp.repeat(d_skip.astype(F32), SSM_HEAD_DIM).reshape(1, d_inner))


def prep_wqkv(wqkv):
    d = wqkv.shape[0]
    nq = N_HEADS * HEAD_DIM
    wq = wqkv[:, :nq].reshape(d, N_KV_HEADS, KV_REP, HEAD_DIM)
    wq = jnp.transpose(wq, (0, 2, 1, 3)).reshape(d, nq)
    return jnp.concatenate([wq, wqkv[:, nq:]], axis=1).astype(BF16)


def attn_layer_prompt(x, g, wqkv, wo, sinks, table, *, n_seq):
    m = x.shape[0]
    nq = N_HEADS * HEAD_DIM
    nkv = N_KV_HEADS * HEAD_DIM
    q, kv = norm_proj(x, g, wqkv, [(nq, BF16), (2 * nkv, F32)])
    o = attn_prompt(q, kv, table, sinks, n_seq=n_seq)
    tail = kv.reshape(n_seq, m // n_seq, 2 * nkv)[:, -WINDOW:]
    k_win = tail[..., :nkv].reshape(n_seq, WINDOW, N_KV_HEADS, HEAD_DIM)
    v_win = tail[..., nkv:].reshape(n_seq, WINDOW, N_KV_HEADS, HEAD_DIM)
    return proj_residual(o, wo, x), k_win, v_win


def ssd_layer_prompt(x, g, ssm, conv_w, conv_b, norm_w, w_out, *, n_seq):
    d_inner = ssm["d_inner"]
    taps = conv_w.shape[0]
    z, xs, bm, cm, dt, conv_st = ssd_in_proj(x, g, ssm["w_z"], ssm["w_xbc"], ssm["w_dt"], conv_w, conv_b,
                                             ssm["dt_bias"], d_inner=d_inner, n_seq=n_seq, stride=1)
    y, h = ssd_prompt(z, xs, bm, cm, dt, ssm["a_log"], ssm["d_exp"], norm_w.reshape(1, d_inner), n_seq=n_seq)
    rep = d_inner // (SSM_GROUPS * SSM_HEAD_DIM)
    h = h.reshape(n_seq, SSM_GROUPS, D_STATE, rep, SSM_HEAD_DIM)
    h = jnp.transpose(h, (0, 1, 3, 4, 2)).reshape(n_seq, SSM_GROUPS * rep, SSM_HEAD_DIM, D_STATE)
    return proj_residual(y, w_out, x), conv_st[:, -(taps - 1):], h


def _to_position_major(state):
    bsz, k, c = state.shape
    return jnp.swapaxes(state, 0, 1).reshape(1, k * bsz, c)


def _from_position_major(state, bsz):
    c = state.shape[-1]
    return jnp.swapaxes(state.reshape(-1, bsz, c), 0, 1)


def attn_layer_sample(x, g, wqkv, wo, sinks, table, cache_k, cache_v, *, bsz):
    m = x.shape[0]
    n_new = m // bsz
    nq = N_HEADS * HEAD_DIM
    nkv = N_KV_HEADS * HEAD_DIM
    win = cache_k.shape[1]
    q, kv = norm_proj(x, g, wqkv, [(nq, BF16), (2 * nkv, F32)])
    q = jnp.transpose(q.reshape(n_new, bsz, KV_REP, nkv), (1, 2, 0, 3))
    q = q.reshape(bsz, KV_REP * n_new, nkv)
    kv = jnp.swapaxes(kv.reshape(n_new, bsz, 2 * nkv), 0, 1)
    o, new_k, new_v = attn_sample(q, kv, cache_k.reshape(bsz, win, nkv), cache_v.reshape(bsz, win, nkv),
                                  table, sinks)
    o = o.reshape(bsz, KV_REP, n_new, N_KV_HEADS, HEAD_DIM)
    o = jnp.transpose(o, (2, 0, 3, 1, 4)).reshape(m, nq)
    return proj_residual(o, wo, x), new_k.reshape(cache_k.shape), new_v.reshape(cache_v.shape)


def ssd_layer_sample(x, g, ssm, conv_w, conv_b, norm_w, w_out, conv_state, ssm_state, *, bsz):
    m = x.shape[0]
    n_new = m // bsz
    d_inner = ssm["d_inner"]
    rep = d_inner // (SSM_GROUPS * SSM_HEAD_DIM)
    z, xs, bm, cm, dt, conv_st = ssd_in_proj(x, g, ssm["w_z"], ssm["w_xbc"], ssm["w_dt"], conv_w, conv_b,
                                             ssm["dt_bias"], d_inner=d_inner, n_seq=1, stride=bsz,
                                             state=_to_position_major(conv_state))

    def slabs(v):
        return v.astype(F32).reshape(n_new, bsz, v.shape[1])

    h0 = ssm_state.reshape(bsz, SSM_GROUPS, rep * SSM_HEAD_DIM, D_STATE)
    y, h = ssd_sample(slabs(z), slabs(xs), slabs(bm), slabs(cm), slabs(dt), h0, ssm["a_log"], ssm["d_exp"],
                      norm_w.reshape(1, d_inner))
    out = proj_residual(y.reshape(m, d_inner), w_out, x)
    return out, _from_position_major(conv_st, bsz), h.reshape(ssm_state.shape)


def kernel(x_prompt, x_sample, cache_k_win, cache_v_win, state_ssm_conv, state_ssm, state_ffn_conv, rel_bias_table, norm_mix, norm_ffn, norm_final, attn_wqkv, attn_wo, attn_sinks, ssm_w_in, ssm_conv_w, ssm_conv_b, ssm_dt_bias, ssm_A_log, ssm_D, ssm_norm, ssm_w_out, ffn_w_up, ffn_conv_w, ffn_conv_b, ffn_w_down):
    bsz, seq, d = x_prompt.shape
    wqkv, wo = prep_wqkv(attn_wqkv[0]), attn_wo[0].astype(BF16)
    w_up, w_down = ffn_w_up.astype(BF16), ffn_w_down.astype(BF16)
    ssm = prep_ssm(ssm_w_in[0], ssm_dt_bias[0], ssm_A_log[0], ssm_D[0])
    w_out = ssm_w_out[0].astype(BF16)
    ffn_taps = ffn_conv_w.shape[1]

    x = x_prompt.reshape(bsz * seq, d)
    x, p_k, p_v = attn_layer_prompt(x, norm_mix[0], wqkv, wo, attn_sinks[0], rel_bias_table, n_seq=bsz)
    x, p_ffn0 = conv_ffn(x, norm_ffn[0], w_up[0], ffn_conv_w[0], ffn_conv_b[0], w_down[0], n_seq=bsz, stride=1)
    x, p_conv, p_ssm = ssd_layer_prompt(x, norm_mix[1], ssm, ssm_conv_w[0], ssm_conv_b[0], ssm_norm[0], w_out,
                                        n_seq=bsz)
    y, p_ffn1 = conv_ffn(x, norm_ffn[1], w_up[1], ffn_conv_w[1], ffn_conv_b[1], w_down[1], n_seq=bsz, stride=1,
                         g_final=norm_final)
    p_ffn = jnp.stack([p_ffn0, p_ffn1])[:, :, -(ffn_taps - 1):]

    dbs, dseq, _ = x_sample.shape
    xs = jnp.swapaxes(x_sample, 0, 1).reshape(dseq * dbs, d)
    xs, s_k, s_v = attn_layer_sample(xs, norm_mix[0], wqkv, wo, attn_sinks[0], rel_bias_table,
                                     cache_k_win[0], cache_v_win[0], bsz=dbs)
    xs, s_ffn0 = conv_ffn(xs, norm_ffn[0], w_up[0], ffn_conv_w[0], ffn_conv_b[0], w_down[0], n_seq=1, stride=dbs,
                          state=_to_position_major(state_ffn_conv[0]))
    xs, s_conv, s_ssm = ssd_layer_sample(xs, norm_mix[1], ssm, ssm_conv_w[0], ssm_conv_b[0], ssm_norm[0], w_out,
                                         state_ssm_conv[0], state_ssm[0], bsz=dbs)
    ys, s_ffn1 = conv_ffn(xs, norm_ffn[1], w_up[1], ffn_conv_w[1], ffn_conv_b[1], w_down[1], n_seq=1, stride=dbs,
                          state=_to_position_major(state_ffn_conv[1]), g_final=norm_final)
    ys = jnp.swapaxes(ys.reshape(dseq, dbs, d), 0, 1)
    s_ffn = jnp.stack([_from_position_major(s_ffn0, dbs), _from_position_major(s_ffn1, dbs)])
    return (y.reshape(bsz, seq, d), ys, p_k[None], p_v[None], p_conv[None], p_ssm[None], p_ffn,
            s_k[None], s_v[None], s_conv[None], s_ssm[None], s_ffn)
```

```python
import functools
import math

import jax
import jax.numpy as jnp
from jax import lax
from jax.experimental import pallas as pl
from jax.experimental.pallas import tpu as pltpu

F32 = jnp.float32
BF16 = jnp.bfloat16

RMS_EPS = 1e-6
N_HEADS = 16
N_KV_HEADS = 4
HEAD_DIM = 64
KV_REP = N_HEADS // N_KV_HEADS
WINDOW = 128
NUM_BUCKETS = 32
MAX_DISTANCE = 128
SSM_HEAD_DIM = 64
SSM_GROUPS = 8
D_STATE = 128
SSD_CHUNK = 128

VMEM_LIMIT_BYTES = 56 * 1024 * 1024
SUBLANES = 8
LANES = 128
ROW_TILE = 512


def _params(*semantics):
    return pltpu.CompilerParams(dimension_semantics=semantics, vmem_limit_bytes=VMEM_LIMIT_BYTES)


def _resident(shape):
    return pl.BlockSpec(shape, lambda *_: (0,) * len(shape), pipeline_mode=pl.Buffered(1))


def _rmsnorm(x, g):
    return x * lax.rsqrt(jnp.mean(x * x, axis=-1, keepdims=True) + RMS_EPS) * g


def _silu(x):
    return x * jax.nn.sigmoid(x)


def _dot(a, b):
    return jnp.dot(a, b, preferred_element_type=F32)


def _dot_nt(a, b):
    return lax.dot_general(a, b, (((1,), (1,)), ((), ())), preferred_element_type=F32)


def _dot_tn(a, b):
    return lax.dot_general(a, b, (((0,), (0,)), ((), ())), preferred_element_type=F32)


def _split3(x):
    hi = x.astype(BF16)
    r1 = x - hi.astype(F32)
    mid = r1.astype(BF16)
    lo = (r1 - mid.astype(F32)).astype(BF16)
    return hi, mid, lo


def _dot_exact_rhs(x, sel):
    hi, mid, lo = _split3(x)
    return _dot(hi, sel) + _dot(mid, sel) + _dot(lo, sel)


def _dot_exact_lhs(sel, x):
    hi, mid, lo = _split3(x)
    return _dot(sel, hi) + _dot(sel, mid) + _dot(sel, lo)


def _norm_proj_kernel(x_ref, g_ref, w_ref, *out_refs):
    h = _rmsnorm(x_ref[...], g_ref[...]).astype(BF16)
    c0 = 0
    for o_ref in out_refs:
        c1 = c0 + o_ref.shape[1]
        o_ref[...] = _dot(h, w_ref[:, c0:c1]).astype(o_ref.dtype)
        c0 = c1


def norm_proj(x, g, w, out_cols_dtypes):
    m, d = x.shape
    tm = min(ROW_TILE, m)
    return pl.pallas_call(
        _norm_proj_kernel,
        grid=(m // tm,),
        in_specs=[pl.BlockSpec((tm, d), lambda i: (i, 0)), _resident((1, d)), _resident(w.shape)],
        out_specs=[pl.BlockSpec((tm, n), lambda i: (i, 0)) for n, _ in out_cols_dtypes],
        out_shape=[jax.ShapeDtypeStruct((m, n), dt) for n, dt in out_cols_dtypes],
        compiler_params=_params("arbitrary"),
        name="norm_proj",
    )(x, g.reshape(1, d), w)


def _proj_residual_kernel(a_ref, w_ref, x_ref, o_ref):
    o_ref[...] = x_ref[...] + _dot(a_ref[...].astype(BF16), w_ref[...])


def proj_residual(a, w, x):
    m, k = a.shape
    d = w.shape[1]
    tm = min(ROW_TILE, m)
    return pl.pallas_call(
        _proj_residual_kernel,
        grid=(m // tm,),
        in_specs=[pl.BlockSpec((tm, k), lambda i: (i, 0)), _resident(w.shape),
                  pl.BlockSpec((tm, d), lambda i: (i, 0))],
        out_specs=pl.BlockSpec((tm, d), lambda i: (i, 0)),
        out_shape=jax.ShapeDtypeStruct((m, d), F32),
        compiler_params=_params("arbitrary"),
        name="proj_residual",
    )(a, w, x)


def _conv_taps(ubuf, cols, cw_ref, cb_ref, c0, width, *, tm, halo, stride, taps):
    y = cb_ref[:, c0:c0 + width]
    for k in range(taps):
        off = halo - (taps - 1 - k) * stride
        y = y + ubuf[off:off + tm, cols] * cw_ref[k:k + 1, c0:c0 + width]
    return y


def _conv_ffn_kernel(*refs, tm, halo, stride, taps, tf, d_ff, has_state, final_norm):
    it = iter(refs)
    x_ref, g_ref, wup_ref, cw_ref, cb_ref, wdn_ref = (next(it) for _ in range(6))
    init_ref = next(it) if has_state else None
    gfin_ref = next(it) if final_norm else None
    o_ref, st_ref, ubufs = next(it), next(it), next(it)

    @pl.when(pl.program_id(1) == 0)
    def _():
        st_ref[0] = init_ref[0] if has_state else jnp.zeros(st_ref.shape[1:], F32)

    x = x_ref[...]
    h = _rmsnorm(x, g_ref[...]).astype(BF16)
    n_chunks = d_ff // tf

    def up_proj(j):
        ubuf = ubufs.at[j % 2]
        for half in range(2):
            c0 = half * d_ff + j * tf
            cols = slice(half * tf, (half + 1) * tf)
            ubuf[0:halo, cols] = st_ref[0, :, c0:c0 + tf]
            ubuf[halo:halo + tm, cols] = _dot(h, wup_ref[:, c0:c0 + tf])

    up_proj(0)
    acc = None
    for j in range(n_chunks):
        if j + 1 < n_chunks:
            up_proj(j + 1)
        ubuf = ubufs.at[j % 2]
        conv = []
        for half in range(2):
            c0 = half * d_ff + j * tf
            cols = slice(half * tf, (half + 1) * tf)
            conv.append(_conv_taps(ubuf, cols, cw_ref, cb_ref, c0, tf,
                                   tm=tm, halo=halo, stride=stride, taps=taps))
            st_ref[0, :, c0:c0 + tf] = ubuf[tm:tm + halo, cols]
        a = (_silu(conv[0]) * conv[1]).astype(BF16)
        part = _dot(a, wdn_ref[j * tf:(j + 1) * tf, :])
        acc = part if acc is None else acc + part
    out = x + acc
    if final_norm:
        out = _rmsnorm(out, gfin_ref[...])
    o_ref[...] = out


def conv_ffn(x, g, w_up, conv_w, conv_b, w_down, *, n_seq, stride, state=None, g_final=None):
    m, d = x.shape
    d_ff = w_down.shape[0]
    taps = conv_w.shape[0]
    rows = m // n_seq
    tm = min(ROW_TILE, rows)
    halo = max(SUBLANES, (taps - 1) * stride)
    tf = 256
    kern = functools.partial(_conv_ffn_kernel, tm=tm, halo=halo, stride=stride, taps=taps, tf=tf,
                             d_ff=d_ff, has_state=state is not None, final_norm=g_final is not None)
    tiles = rows // tm
    row_spec = pl.BlockSpec((tm, d), lambda s, t: (s * tiles + t, 0))
    st_spec = pl.BlockSpec((1, halo, 2 * d_ff), lambda s, t: (s, 0, 0))
    in_specs = [row_spec, _resident((1, d)), _resident(w_up.shape), _resident(conv_w.shape),
                _resident((1, 2 * d_ff)), _resident(w_down.shape)]
    args = [x, g.reshape(1, d), w_up, conv_w, conv_b.reshape(1, 2 * d_ff), w_down]
    if state is not None:
        in_specs.append(st_spec)
        args.append(state)
    if g_final is not None:
        in_specs.append(_resident((1, d)))
        args.append(g_final.reshape(1, d))
    return pl.pallas_call(
        kern,
        grid=(n_seq, tiles),
        in_specs=in_specs,
        out_specs=[row_spec, st_spec],
        out_shape=[jax.ShapeDtypeStruct((m, d), F32),
                   jax.ShapeDtypeStruct((n_seq, halo, 2 * d_ff), F32)],
        scratch_shapes=[pltpu.VMEM((2, halo + tm, 2 * tf), F32)],
        compiler_params=_params("arbitrary", "arbitrary"),
        name="conv_ffn",
    )(*args)


def _softplus(x):
    return jnp.maximum(x, 0.0) + jnp.log1p(jnp.exp(-jnp.abs(x)))


def _ssd_in_proj_kernel(*refs, tm, halo, stride, taps, tc, has_state):
    it = iter(refs)
    x_ref, g_ref, w_ref, cw_ref, cb_ref, dtb_ref = (next(it) for _ in range(6))
    init_ref = next(it) if has_state else None
    z_ref, xs_ref, b_ref, c_ref, dt_ref, st_ref, ubufs = (next(it) for _ in range(7))

    @pl.when(pl.program_id(1) == 0)
    def _():
        st_ref[0] = init_ref[0] if has_state else jnp.zeros(st_ref.shape[1:], F32)

    h = _rmsnorm(x_ref[...], g_ref[...]).astype(BF16)
    d_inner = z_ref.shape[1]
    conv_dim = st_ref.shape[2]
    n_heads = w_ref.shape[1] - d_inner - conv_dim
    n_chunks = conv_dim // tc

    def project(j):
        ubuf = ubufs.at[j % 2]
        ubuf[0:halo, :] = st_ref[0, :, j * tc:(j + 1) * tc]
        ubuf[halo:halo + tm, :] = _dot(h, w_ref[:, d_inner + j * tc:d_inner + (j + 1) * tc])

    project(0)
    z_ref[...] = _dot(h, w_ref[:, 0:d_inner])
    dt = _softplus(_dot(h, w_ref[:, d_inner + conv_dim:]) + dtb_ref[:, 0:n_heads])
    dt_ref[...] = jnp.concatenate([dt, jnp.zeros((tm, dt_ref.shape[1] - n_heads), F32)], axis=1)
    outs = ((xs_ref, 0), (b_ref, xs_ref.shape[1]), (c_ref, xs_ref.shape[1] + b_ref.shape[1]))
    for j in range(n_chunks):
        if j + 1 < n_chunks:
            project(j + 1)
        ubuf = ubufs.at[j % 2]
        c0 = j * tc
        y = _silu(_conv_taps(ubuf, slice(None), cw_ref, cb_ref, c0, tc,
                             tm=tm, halo=halo, stride=stride, taps=taps))
        st_ref[0, :, c0:c0 + tc] = ubuf[tm:tm + halo, :]
        o_ref, base = [(r, b0) for r, b0 in outs if b0 <= c0][-1]
        o_ref[:, c0 - base:c0 - base + tc] = y.astype(o_ref.dtype)


def ssd_in_proj(x, g, w_in, conv_w, conv_b, dt_bias, *, d_inner, n_seq, stride, state=None):
    m, d = x.shape
    conv_dim = conv_w.shape[1]
    gn = (conv_dim - d_inner) // 2
    taps = conv_w.shape[0]
    rows = m // n_seq
    tm = min(ROW_TILE, rows)
    halo = max(SUBLANES, (taps - 1) * stride)
    tc = 512
    kern = functools.partial(_ssd_in_proj_kernel, tm=tm, halo=halo, stride=stride, taps=taps, tc=tc,
                             has_state=state is not None)
    tiles = rows // tm

    def rows_of(n):
        return pl.BlockSpec((tm, n), lambda s, t: (s * tiles + t, 0))

    st_spec = pl.BlockSpec((1, halo, conv_dim), lambda s, t: (s, 0, 0))
    in_specs = [rows_of(d), _resident((1, d)), _resident(w_in.shape), _resident(conv_w.shape),
                _resident((1, conv_dim)), _resident((1, LANES))]
    args = [x, g.reshape(1, d), w_in, conv_w, conv_b.reshape(1, conv_dim), dt_bias]
    if state is not None:
        in_specs.append(st_spec)
        args.append(state)
    return pl.pallas_call(
        kern,
        grid=(n_seq, tiles),
        in_specs=in_specs,
        out_specs=[rows_of(d_inner), rows_of(d_inner), rows_of(gn), rows_of(gn), rows_of(LANES), st_spec],
        out_shape=[jax.ShapeDtypeStruct((m, d_inner), F32), jax.ShapeDtypeStruct((m, d_inner), F32),
                   jax.ShapeDtypeStruct((m, gn), BF16), jax.ShapeDtypeStruct((m, gn), BF16),
                   jax.ShapeDtypeStruct((m, LANES), F32),
                   jax.ShapeDtypeStruct((n_seq, halo, conv_dim), F32)],
        scratch_shapes=[pltpu.VMEM((2, halo + tm, tc), F32)],
        compiler_params=_params("arbitrary", "arbitrary"),
        name="ssd_in_proj",
    )(*args)


def _block_diag_rows(a, rep, width):
    col_block = lax.broadcasted_iota(jnp.int32, a.shape, 1) // width
    zero = jnp.zeros_like(a)
    return jnp.concatenate([jnp.where(col_block == r, a, zero) for r in range(rep)], axis=0)


def _head_expander(n_heads, width):
    row = lax.broadcasted_iota(jnp.int32, (LANES, n_heads * width), 0)
    col = lax.broadcasted_iota(jnp.int32, (LANES, n_heads * width), 1)
    return jnp.where(row == col // width, 1.0, 0.0).astype(BF16)


def _ssd_prompt_kernel(z_ref, xs_ref, b_ref, c_ref, dt_ref, alog_ref, dexp_ref, nw_ref,
                       y_ref, h_ref, *, groups, rep, hd, n_state):
    t = dt_ref.shape[0]
    gw = rep * hd

    @pl.when(pl.program_id(1) == 0)
    def _():
        h_ref[...] = jnp.zeros(h_ref.shape, F32)

    dt = dt_ref[...]
    da = dt * (-jnp.exp(alog_ref[...]))
    row = lax.broadcasted_iota(jnp.int32, (t, t), 0)
    col = lax.broadcasted_iota(jnp.int32, (t, t), 1)
    causal = row >= col
    cs = _dot_exact_lhs(jnp.where(causal, 1.0, 0.0).astype(BF16), da)
    cs_t = cs.T
    dt_t = dt.T
    dd_t = jnp.exp(cs_t[:, t - 1:t] - cs_t) * dt_t
    chunk_decay = jnp.exp(jnp.broadcast_to(cs[t - 1:t, :], (SUBLANES, LANES)))
    cdx = _dot_exact_rhs(chunk_decay, _head_expander(groups * rep, hd))[0:1, :]

    gsl = [slice(g * gw, (g + 1) * gw) for g in range(groups)]
    nsl = [slice(g * n_state, (g + 1) * n_state) for g in range(groups)]
    cbs, b_ts, cg32s, xgs, x_bds, hgs, h_bds = [], [], [], [], [], [], []
    for g in range(groups):
        bg = b_ref[:, nsl[g]]
        cg = c_ref[:, nsl[g]]
        cbs.append(_dot_nt(cg, bg))
        b_ts.append(bg.astype(F32).T)
        cg32s.append(cg.astype(F32))
        xg = xs_ref[:, gsl[g]]
        xgs.append(xg)
        x_bds.append(_block_diag_rows(xg.astype(BF16), rep, hd))
        hg = h_ref[0, g]
        hgs.append(hg)
        h_bds.append(_block_diag_rows(hg.astype(BF16), rep, hd))
    lhs, s_lhs = [], []
    for g in range(groups):
        w_parts, c_parts, s_parts = [], [], []
        for r in range(rep):
            hh = g * rep + r
            cs_col = cs[:, hh:hh + 1]
            seg = cs_col - cs_t[hh:hh + 1, :]
            lmat = jnp.exp(jnp.where(causal, seg, -jnp.inf))
            w_parts.append((cbs[g] * lmat * dt_t[hh:hh + 1, :]).astype(BF16))
            c_parts.append((cg32s[g] * jnp.exp(cs_col)).astype(BF16))
            s_parts.append((b_ts[g] * dd_t[hh:hh + 1, :]).astype(BF16))
        lhs.append(jnp.concatenate(w_parts + c_parts, axis=1))
        s_lhs.append(jnp.concatenate(s_parts, axis=1))
    ys = []
    for g in range(groups):
        ys.append(_dot(lhs[g], jnp.concatenate([x_bds[g], h_bds[g]], axis=0)))
        h_ref[0, g] = hgs[g] * cdx[:, gsl[g]] + _dot(s_lhs[g], x_bds[g])
    for g in range(groups):
        y = (ys[g] + dexp_ref[:, gsl[g]] * xgs[g]) * _silu(z_ref[:, gsl[g]])
        y = y * lax.rsqrt(jnp.mean(y * y, axis=-1, keepdims=True) + RMS_EPS) * nw_ref[:, gsl[g]]
        y_ref[:, gsl[g]] = y.astype(y_ref.dtype)


def ssd_prompt(z, xs, bm, cm, dt, a_log, d_exp, norm_w, *, n_seq):
    m, d_inner = xs.shape
    gn = bm.shape[1]
    groups, n_state, hd = SSM_GROUPS, D_STATE, SSM_HEAD_DIM
    rep = d_inner // (groups * hd)
    t = SSD_CHUNK
    chunks = m // n_seq // t
    kern = functools.partial(_ssd_prompt_kernel, groups=groups, rep=rep, hd=hd, n_state=n_state)

    def rows_of(n):
        return pl.BlockSpec((t, n), lambda s, c: (s * chunks + c, 0))

    h_spec = pl.BlockSpec((1, groups, n_state, rep * hd), lambda s, c: (s, 0, 0, 0))
    return pl.pallas_call(
        kern,
        grid=(n_seq, chunks),
        in_specs=[rows_of(d_inner), rows_of(d_inner), rows_of(gn), rows_of(gn), rows_of(LANES),
                  _resident((1, LANES)), _resident((1, d_inner)), _resident((1, d_inner))],
        out_specs=[rows_of(d_inner), h_spec],
        out_shape=[jax.ShapeDtypeStruct((m, d_inner), BF16),
                   jax.ShapeDtypeStruct((n_seq, groups, n_state, rep * hd), F32)],
        compiler_params=_params("arbitrary", "arbitrary"),
        name="ssd_prompt",
    )(z, xs, bm, cm, dt, a_log, d_exp, norm_w)


def _bias_from_buckets(bucket, table_ref, head):
    acc = jnp.zeros(bucket.shape, F32)
    for j in range(NUM_BUCKETS):
        acc = jnp.where(bucket == j, table_ref[j, head], acc)
    return jnp.where(bucket >= 0, acc, -jnp.inf)


def _sink_softmax(s, sink):
    m = jnp.maximum(jnp.max(s, axis=-1, keepdims=True), sink)
    p = jnp.exp(s - m)
    return p / (jnp.sum(p, axis=-1, keepdims=True) + jnp.exp(sink - m))


def _attn_prompt_kernel(table_ref, sinks_ref, bucket_ref, q_ref, kvp_ref, kvo_ref, o_ref, bias_ref):
    blk = q_ref.shape[0]
    gd = N_KV_HEADS * HEAD_DIM
    first_block = pl.program_id(1) == 0

    @pl.when((pl.program_id(0) == 0) & first_block)
    def _():
        bucket = bucket_ref[...]
        own_key = lax.broadcasted_iota(jnp.int32, bucket.shape, 0) >= blk
        for hh in range(N_HEADS):
            g, r = divmod(hh, KV_REP)
            bias = _bias_from_buckets(bucket, table_ref, hh)
            bias_ref[0, g, :, r * blk:(r + 1) * blk] = bias
            bias_ref[1, g, :, r * blk:(r + 1) * blk] = jnp.where(own_key, bias, -jnp.inf)

    kv = jnp.concatenate([kvp_ref[...], kvo_ref[...]], axis=0)
    k16 = kv[:, :gd].astype(BF16)
    v_t = kv[:, gd:].T.astype(BF16)
    which_bias = jnp.where(first_block, 1, 0)
    lane_group = lax.broadcasted_iota(jnp.int32, (blk, gd), 1) // HEAD_DIM
    q = q_ref[...] * jnp.asarray(HEAD_DIM ** -0.5, q_ref.dtype)
    groups = range(N_KV_HEADS)

    scores, sinks = [], []
    for g in groups:
        q_g = jnp.concatenate(
            [jnp.where(lane_group == g, q[:, r * gd:(r + 1) * gd], jnp.zeros((), q.dtype))
             for r in range(KV_REP)], axis=0)
        scores.append(_dot_nt(k16, q_g) + bias_ref[which_bias, g])
        sinks.append(jnp.concatenate([jnp.full((1, blk), sinks_ref[g * KV_REP + r], F32)
                                      for r in range(KV_REP)], axis=1))
    maxes = [jnp.maximum(jnp.max(scores[g], axis=0, keepdims=True), sinks[g]) for g in groups]
    probs = [jnp.exp(scores[g] - maxes[g]) for g in groups]
    denoms = [jnp.sum(probs[g], axis=0, keepdims=True) + jnp.exp(sinks[g] - maxes[g]) for g in groups]
    zeros = jnp.zeros((HEAD_DIM, 2 * blk), BF16)
    for g in groups:
        p = (probs[g] * (1.0 / denoms[g])).astype(BF16)
        p_rows = jnp.concatenate([p[:, r * blk:(r + 1) * blk] for r in range(KV_REP)], axis=0)
        v_g = v_t[g * HEAD_DIM:(g + 1) * HEAD_DIM, :]
        v_bd = jnp.concatenate(
            [jnp.concatenate([v_g if rr == r else zeros for rr in range(KV_REP)], axis=1)
             for r in range(KV_REP)], axis=0)
        o_t = _dot(v_bd, p_rows)
        o_ref[:, g * KV_REP * HEAD_DIM:(g + 1) * KV_REP * HEAD_DIM] = o_t.T.astype(o_ref.dtype)


def _t5_bucket(dist):
    max_exact = NUM_BUCKETS // 2
    nf = jnp.maximum(dist, 1).astype(F32)
    large = max_exact + (jnp.log(nf / max_exact) / math.log(MAX_DISTANCE / max_exact)
                         * (NUM_BUCKETS - max_exact)).astype(jnp.int32)
    large = jnp.minimum(large, NUM_BUCKETS - 1)
    return jnp.where(dist < max_exact, dist, large)


def attn_prompt(q, kv, table, sinks, *, n_seq):
    m = q.shape[0]
    blk = WINDOW
    nb = m // n_seq // blk
    dist = (jnp.arange(blk)[None, :] + blk) - jnp.arange(2 * blk)[:, None]
    bucket = jnp.where((dist >= 0) & (dist < WINDOW), _t5_bucket(jnp.maximum(dist, 0)), -1).astype(jnp.int32)
    smem = pl.BlockSpec(memory_space=pltpu.SMEM)
    return pl.pallas_call(
        _attn_prompt_kernel,
        grid=(n_seq, nb),
        in_specs=[smem, smem, _resident(bucket.shape),
                  pl.BlockSpec((blk, q.shape[1]), lambda s, c: (s * nb + c, 0)),
                  pl.BlockSpec((blk, kv.shape[1]), lambda s, c: (s * nb + jnp.maximum(c - 1, 0), 0)),
                  pl.BlockSpec((blk, kv.shape[1]), lambda s, c: (s * nb + c, 0))],
        out_specs=pl.BlockSpec((blk, q.shape[1]), lambda s, c: (s * nb + c, 0)),
        out_shape=jax.ShapeDtypeStruct(q.shape, BF16),
        scratch_shapes=[pltpu.VMEM((2, N_KV_HEADS, 2 * blk, KV_REP * blk), F32)],
        compiler_params=_params("arbitrary", "arbitrary"),
        name="attn_prompt",
    )(table, sinks, bucket, q, kv, kv)


def _attn_sample_kernel(table_ref, bucket_ref, sink_ref, q_ref, kvn_ref, ck_ref, cv_ref,
                        o_ref, nk_ref, nv_ref, bias_ref, kall, vall, *, n_new):
    bb, win, gd = ck_ref.shape
    rows = bias_ref.shape[0]
    per_group = rows // N_KV_HEADS
    pad = kvn_ref.shape[1]

    @pl.when(pl.program_id(0) == 0)
    def _():
        for hh in range(N_HEADS):
            rs = slice(hh * n_new, (hh + 1) * n_new)
            bias_ref[rs, :] = _bias_from_buckets(bucket_ref[rs, :], table_ref, hh)
        tail = kall.shape[1] - win - pad
        kall[:, win + pad:, :] = jnp.zeros((bb, tail, gd), F32)
        vall[:, win + pad:, :] = jnp.zeros((bb, tail, gd), F32)

    row_group = lax.broadcasted_iota(jnp.int32, (rows, gd), 0) // per_group
    col_group = lax.broadcasted_iota(jnp.int32, (rows, gd), 1) // HEAD_DIM
    own = row_group == col_group
    scale = jnp.asarray(HEAD_DIM ** -0.5, q_ref.dtype)
    seqs = range(bb)
    for i in seqs:
        kall[i, 0:win, :] = ck_ref[i]
        kall[i, win:win + pad, :] = kvn_ref[i, :, 0:gd]
        vall[i, 0:win, :] = cv_ref[i]
        vall[i, win:win + pad, :] = kvn_ref[i, :, gd:2 * gd]
    for i in seqs:
        nk_ref[i] = kall[i, n_new:n_new + win, :]
        nv_ref[i] = vall[i, n_new:n_new + win, :]
    scores = []
    for i in seqs:
        qi = q_ref[i] * scale
        q_bd = jnp.where(own, jnp.concatenate([qi] * N_KV_HEADS, axis=0), jnp.zeros((), qi.dtype))
        scores.append(_dot_nt(q_bd, kall[i].astype(BF16)) + bias_ref[...])
    sink = sink_ref[...]
    maxes = [jnp.maximum(jnp.max(s, axis=-1, keepdims=True), sink) for s in scores]
    probs = [jnp.exp(scores[i] - maxes[i]) for i in seqs]
    denoms = [jnp.sum(probs[i], axis=-1, keepdims=True) + jnp.exp(sink - maxes[i]) for i in seqs]
    outs = [jnp.where(own, _dot((probs[i] * (1.0 / denoms[i])).astype(BF16), vall[i].astype(BF16)), 0.0)
            for i in seqs]
    for i in seqs:
        o_ref[i] = sum(outs[i][g * per_group:(g + 1) * per_group] for g in range(N_KV_HEADS)).astype(o_ref.dtype)


def attn_sample(q, kv_new, cache_k, cache_v, table, sinks):
    bsz, win, gd = cache_k.shape
    n_new = kv_new.shape[1]
    rows = N_HEADS * n_new
    keys = 2 * LANES
    pad = SUBLANES
    bb = 8
    kv_new = jnp.pad(kv_new, ((0, 0), (0, pad - n_new), (0, 0)))
    dist = (win + jnp.arange(n_new))[:, None] - jnp.arange(keys)[None, :]
    ok = (dist >= 0) & (dist < WINDOW) & (jnp.arange(keys)[None, :] < win + n_new)
    bucket = jnp.where(ok, _t5_bucket(jnp.maximum(dist, 0)), -1).astype(jnp.int32)
    bucket = jnp.tile(bucket, (N_HEADS, 1))
    sink_rows = jnp.repeat(sinks.astype(F32), n_new).reshape(rows, 1)
    smem = pl.BlockSpec(memory_space=pltpu.SMEM)

    def per_seq(shape):
        return pl.BlockSpec((bb,) + shape, lambda i: (i, 0, 0))

    return pl.pallas_call(
        functools.partial(_attn_sample_kernel, n_new=n_new),
        grid=(bsz // bb,),
        in_specs=[smem, _resident(bucket.shape), _resident(sink_rows.shape), per_seq(q.shape[1:]),
                  per_seq((pad, 2 * gd)), per_seq((win, gd)), per_seq((win, gd))],
        out_specs=[per_seq(q.shape[1:]), per_seq((win, gd)), per_seq((win, gd))],
        out_shape=[jax.ShapeDtypeStruct(q.shape, BF16), jax.ShapeDtypeStruct(cache_k.shape, F32),
                   jax.ShapeDtypeStruct(cache_v.shape, F32)],
        scratch_shapes=[pltpu.VMEM((rows, keys), F32), pltpu.VMEM((bb, keys, gd), F32),
                        pltpu.VMEM((bb, keys, gd), F32)],
        compiler_params=_params("arbitrary"),
        name="attn_sample",
    )(table, bucket, sink_rows, q, kv_new, cache_k, cache_v)


SEQ_BLOCK = 8
SLOT = 16


def _to_slots(slabs, bb):
    c = slabs[0].shape[1]
    stack = jnp.concatenate(list(slabs) + [jnp.zeros(((SUBLANES - len(slabs)) * bb, c), F32)], axis=0)
    row = lax.broadcasted_iota(jnp.int32, (bb * SLOT, SUBLANES * bb), 0)
    col = lax.broadcasted_iota(jnp.int32, (bb * SLOT, SUBLANES * bb), 1)
    place = jnp.where((row // SLOT == col % bb) & (row % SLOT == col // bb), 1.0, 0.0).astype(BF16)
    return _dot(place, stack.astype(BF16))


def _ssd_sample_kernel(z_ref, xs_ref, b_ref, c_ref, dt_ref, h0_ref, alog_ref, dexp_ref, nw_ref,
                       y_ref, hn_ref, cl_scr, yo_scr, ltt_scr, rhs_scr, *, groups, rep, hd, n_state):
    n_pos, bb, d_inner = xs_ref.shape
    gw = rep * hd

    a = -jnp.exp(alog_ref[...])
    expander = _head_expander(groups * rep, hd)
    dt = [dt_ref[l] for l in range(n_pos)]
    cs, run = [], None
    for l in range(n_pos):
        run = dt[l] * a if run is None else run + dt[l] * a
        cs.append(run)
    csx = [_dot_exact_rhs(v, expander) for v in cs]
    dtx = [_dot_exact_rhs(v, expander) for v in dt]
    x = [xs_ref[l] for l in range(n_pos)]
    bm = [b_ref[l] for l in range(n_pos)]
    cm = [c_ref[l] for l in range(n_pos)]

    xw = [x[s] * (jnp.exp(csx[n_pos - 1] - csx[s]) * dtx[s]) for s in range(n_pos)]
    decay_parts = [part.astype(F32) for part in _split3(jnp.exp(csx[n_pos - 1]))]
    lt = _to_slots(xw + decay_parts, bb)
    rb = _to_slots(bm, bb)
    cl_scr[...] = _to_slots(cm, bb).astype(BF16)

    slot_of_row = lax.broadcasted_iota(jnp.int32, (bb * SLOT, n_state), 0) % SLOT
    ones_rows = jnp.where((slot_of_row >= n_pos) & (slot_of_row < n_pos + 3), 1.0, 0.0)
    for g in range(groups):
        ltt_scr[g] = lt[:, g * gw:(g + 1) * gw].T.astype(BF16)
        rhs_scr[g] = jnp.concatenate([rb[:, g * n_state:(g + 1) * n_state], ones_rows],
                                     axis=1).astype(BF16)

    seq_of_row = lax.broadcasted_iota(jnp.int32, (bb * SLOT, 2 * n_state), 0) // SLOT

    def per_sequence(i, carry):
        r0 = pl.multiple_of(i * SLOT, SLOT)
        mine = seq_of_row == i
        for g in range(groups):
            h0 = h0_ref[i, g]
            cq = cl_scr[pl.ds(r0, SLOT), g * n_state:(g + 1) * n_state]
            yo_scr[pl.ds(r0, SLOT), g * gw:(g + 1) * gw] = _dot_nt(cq, h0.astype(BF16))
            rhs = rhs_scr[g]
            sd = _dot(ltt_scr[g], jnp.where(mine, rhs, jnp.zeros_like(rhs)))
            hn_ref[i, g] = h0 * sd[:, n_state:] + sd[:, :n_state]
        return carry

    lax.fori_loop(0, bb, per_sequence, 0)

    yo_parts = _split3(yo_scr[...])
    pick_row = lax.broadcasted_iota(jnp.int32, (bb, bb * SLOT), 0)
    pick_col = lax.broadcasted_iota(jnp.int32, (bb, bb * SLOT), 1)
    for l in range(n_pos):
        pick = jnp.where(pick_col == pick_row * SLOT + l, 1.0, 0.0).astype(BF16)
        y_off = _dot(pick, yo_parts[0]) + _dot(pick, yo_parts[1]) + _dot(pick, yo_parts[2])
        acc = None
        for s in range(l + 1):
            prod = cm[l] * bm[s]
            cbx = jnp.concatenate(
                [jnp.broadcast_to(jnp.sum(prod[:, g * n_state:(g + 1) * n_state], axis=-1, keepdims=True),
                                  (bb, gw)) for g in range(groups)], axis=1)
            term = cbx * jnp.exp(csx[l] - csx[s]) * dtx[s] * x[s]
            acc = term if acc is None else acc + term
        y = acc + y_off * jnp.exp(csx[l])
        y = (y + dexp_ref[...] * x[l]) * _silu(z_ref[l])
        normed = []
        for g in range(groups):
            yg = y[:, g * gw:(g + 1) * gw]
            normed.append(yg * lax.rsqrt(jnp.mean(yg * yg, axis=-1, keepdims=True) + RMS_EPS))
        y_ref[l] = jnp.concatenate(normed, axis=1) * nw_ref[...]


def ssd_sample(z, xs, bm, cm, dt, h0, a_log, d_exp, norm_w):
    n_pos, bsz, d_inner = xs.shape
    groups, n_state, hd = SSM_GROUPS, D_STATE, SSM_HEAD_DIM
    rep = d_inner // (groups * hd)
    bb = SEQ_BLOCK
    kern = functools.partial(_ssd_sample_kernel, groups=groups, rep=rep, hd=hd, n_state=n_state)

    def slab(n):
        return pl.BlockSpec((n_pos, bb, n), lambda i: (0, i, 0))

    h_spec = pl.BlockSpec((bb,) + h0.shape[1:], lambda i: (i, 0, 0, 0))
    rows = bb * SLOT
    return pl.pallas_call(
        kern,
        grid=(bsz // bb,),
        in_specs=[slab(d_inner), slab(d_inner), slab(bm.shape[2]), slab(cm.shape[2]), slab(LANES), h_spec,
                  _resident((1, LANES)), _resident((1, d_inner)), _resident((1, d_inner))],
        out_specs=[slab(d_inner), h_spec],
        out_shape=[jax.ShapeDtypeStruct(xs.shape, F32), jax.ShapeDtypeStruct(h0.shape, F32)],
        scratch_shapes=[pltpu.VMEM((rows, groups * n_state), BF16), pltpu.VMEM((rows, d_inner), F32),
                        pltpu.VMEM((groups, rep * hd, rows), BF16),
                        pltpu.VMEM((groups, rows, 2 * n_state), BF16)],
        compiler_params=_params("arbitrary"),
        name="ssd_sample",
    )(z, xs, bm, cm, dt, h0, a_log, d_exp, norm_w)


def _pad_lanes(v):
    return jnp.pad(v.astype(F32), (0, LANES - v.shape[0])).reshape(1, LANES)


def prep_ssm(w_in, dt_bias, a_log, d_skip):
    n_heads = dt_bias.shape[0]
    d_inner = n_heads * SSM_HEAD_DIM
    return dict(
        d_inner=d_inner,
        w_in=w_in.astype(BF16),
        dt_bias=_pad_lanes(dt_bias),
        a_log=_pad_lanes(a_log),
        d_exp=jnp.repeat(d_skip.astype(F32), SSM_HEAD_DIM).reshape(1, d_inner))


def prep_wqkv(wqkv):
    d = wqkv.shape[0]
    nq = N_HEADS * HEAD_DIM
    wq = wqkv[:, :nq].reshape(d, N_KV_HEADS, KV_REP, HEAD_DIM)
    wq = jnp.transpose(wq, (0, 2, 1, 3)).reshape(d, nq)
    return jnp.concatenate([wq, wqkv[:, nq:]], axis=1).astype(BF16)


def attn_layer_prompt(x, g, wqkv, wo, sinks, table, *, n_seq):
    m = x.shape[0]
    nq = N_HEADS * HEAD_DIM
    nkv = N_KV_HEADS * HEAD_DIM
    q, kv = norm_proj(x, g, wqkv, [(nq, BF16), (2 * nkv, F32)])
    o = attn_prompt(q, kv, table, sinks, n_seq=n_seq)
    tail = kv.reshape(n_seq, m // n_seq, 2 * nkv)[:, -WINDOW:]
    k_win = tail[..., :nkv].reshape(n_seq, WINDOW, N_KV_HEADS, HEAD_DIM)
    v_win = tail[..., nkv:].reshape(n_seq, WINDOW, N_KV_HEADS, HEAD_DIM)
    return proj_residual(o, wo, x), k_win, v_win


def ssd_layer_prompt(x, g, ssm, conv_w, conv_b, norm_w, w_out, *, n_seq):
    d_inner = ssm["d_inner"]
    taps = conv_w.shape[0]
    z, xs, bm, cm, dt, conv_st = ssd_in_proj(x, g, ssm["w_in"], conv_w, conv_b,
                                             ssm["dt_bias"], d_inner=d_inner, n_seq=n_seq, stride=1)
    y, h = ssd_prompt(z, xs, bm, cm, dt, ssm["a_log"], ssm["d_exp"], norm_w.reshape(1, d_inner), n_seq=n_seq)
    rep = d_inner // (SSM_GROUPS * SSM_HEAD_DIM)
    h = h.reshape(n_seq, SSM_GROUPS, D_STATE, rep, SSM_HEAD_DIM)
    h = jnp.transpose(h, (0, 1, 3, 4, 2)).reshape(n_seq, SSM_GROUPS * rep, SSM_HEAD_DIM, D_STATE)
    return proj_residual(y, w_out, x), conv_st[:, -(taps - 1):], h


def _to_position_major(state):
    bsz, k, c = state.shape
    return jnp.swapaxes(state, 0, 1).reshape(1, k * bsz, c)


def _from_position_major(state, bsz):
    c = state.shape[-1]
    return jnp.swapaxes(state.reshape(-1, bsz, c), 0, 1)


def attn_layer_sample(x, g, wqkv, wo, sinks, table, cache_k, cache_v, *, bsz):
    m = x.shape[0]
    n_new = m // bsz
    nq = N_HEADS * HEAD_DIM
    nkv = N_KV_HEADS * HEAD_DIM
    win = cache_k.shape[1]
    q, kv = norm_proj(x, g, wqkv, [(nq, BF16), (2 * nkv, F32)])
    q = jnp.transpose(q.reshape(n_new, bsz, KV_REP, nkv), (1, 2, 0, 3))
    q = q.reshape(bsz, KV_REP * n_new, nkv)
    kv = jnp.swapaxes(kv.reshape(n_new, bsz, 2 * nkv), 0, 1)
    o, new_k, new_v = attn_sample(q, kv, cache_k.reshape(bsz, win, nkv), cache_v.reshape(bsz, win, nkv),
                                  table, sinks)
    o = o.reshape(bsz, KV_REP, n_new, N_KV_HEADS, HEAD_DIM)
    o = jnp.transpose(o, (2, 0, 3, 1, 4)).reshape(m, nq)
    return proj_residual(o, wo, x), new_k.reshape(cache_k.shape), new_v.reshape(cache_v.shape)


def ssd_layer_sample(x, g, ssm, conv_w, conv_b, norm_w, w_out, conv_state, ssm_state, *, bsz):
    m = x.shape[0]
    n_new = m // bsz
    d_inner = ssm["d_inner"]
    rep = d_inner // (SSM_GROUPS * SSM_HEAD_DIM)
    z, xs, bm, cm, dt, conv_st = ssd_in_proj(x, g, ssm["w_in"], conv_w, conv_b,
                                             ssm["dt_bias"], d_inner=d_inner, n_seq=1, stride=bsz,
                                             state=_to_position_major(conv_state))

    def slabs(v):
        return v.astype(F32).reshape(n_new, bsz, v.shape[1])

    h0 = ssm_state.reshape(bsz, SSM_GROUPS, rep * SSM_HEAD_DIM, D_STATE)
    y, h = ssd_sample(slabs(z), slabs(xs), slabs(bm), slabs(cm), slabs(dt), h0, ssm["a_log"], ssm["d_exp"],
                      norm_w.reshape(1, d_inner))
    out = proj_residual(y.reshape(m, d_inner), w_out, x)
    return out, _from_position_major(conv_st, bsz), h.reshape(ssm_state.shape)


def kernel(x_prompt, x_sample, cache_k_win, cache_v_win, state_ssm_conv, state_ssm, state_ffn_conv, rel_bias_table, norm_mix, norm_ffn, norm_final, attn_wqkv, attn_wo, attn_sinks, ssm_w_in, ssm_conv_w, ssm_conv_b, ssm_dt_bias, ssm_A_log, ssm_D, ssm_norm, ssm_w_out, ffn_w_up, ffn_conv_w, ffn_conv_b, ffn_w_down):
    bsz, seq, d = x_prompt.shape
    wqkv, wo = prep_wqkv(attn_wqkv[0]), attn_wo[0].astype(BF16)
    w_up = [w.astype(BF16) for w in ffn_w_up]
    w_down = [w.astype(BF16) for w in ffn_w_down]
    ssm = prep_ssm(ssm_w_in[0], ssm_dt_bias[0], ssm_A_log[0], ssm_D[0])
    w_out = ssm_w_out[0].astype(BF16)
    ffn_taps = ffn_conv_w.shape[1]

    x = x_prompt.reshape(bsz * seq, d)
    x, p_k, p_v = attn_layer_prompt(x, norm_mix[0], wqkv, wo, attn_sinks[0], rel_bias_table, n_seq=bsz)
    x, p_ffn0 = conv_ffn(x, norm_ffn[0], w_up[0], ffn_conv_w[0], ffn_conv_b[0], w_down[0], n_seq=bsz, stride=1)
    x, p_conv, p_ssm = ssd_layer_prompt(x, norm_mix[1], ssm, ssm_conv_w[0], ssm_conv_b[0], ssm_norm[0], w_out,
                                        n_seq=bsz)
    y, p_ffn1 = conv_ffn(x, norm_ffn[1], w_up[1], ffn_conv_w[1], ffn_conv_b[1], w_down[1], n_seq=bsz, stride=1,
                         g_final=norm_final)
    p_ffn = jnp.stack([p_ffn0, p_ffn1])[:, :, -(ffn_taps - 1):]

    dbs, dseq, _ = x_sample.shape
    xs = jnp.swapaxes(x_sample, 0, 1).reshape(dseq * dbs, d)
    xs, s_k, s_v = attn_layer_sample(xs, norm_mix[0], wqkv, wo, attn_sinks[0], rel_bias_table,
                                     cache_k_win[0], cache_v_win[0], bsz=dbs)
    xs, s_ffn0 = conv_ffn(xs, norm_ffn[0], w_up[0], ffn_conv_w[0], ffn_conv_b[0], w_down[0], n_seq=1, stride=dbs,
                          state=_to_position_major(state_ffn_conv[0]))
    xs, s_conv, s_ssm = ssd_layer_sample(xs, norm_mix[1], ssm, ssm_conv_w[0], ssm_conv_b[0], ssm_norm[0], w_out,
                                         state_ssm_conv[0], state_ssm[0], bsz=dbs)
    ys, s_ffn1 = conv_ffn(xs, norm_ffn[1], w_up[1], ffn_conv_w[1], ffn_conv_b[1], w_down[1], n_seq=1, stride=dbs,
                          state=_to_position_major(state_ffn_conv[1]), g_final=norm_final)
    ys = jnp.swapaxes(ys.reshape(dseq, dbs, d), 0, 1)
    s_ffn = jnp.stack([_from_position_major(s_ffn0, dbs), _from_position_major(s_ffn1, dbs)])
    return (y.reshape(bsz, seq, d), ys, p_k[None], p_v[None], p_conv[None], p_ssm[None], p_ffn,
            s_k[None], s_v[None], s_conv[None], s_ssm[None], s_ffn)
```
